```python
import math
import jax, jax.numpy as jnp
from jax import lax
import numpy as np

D_MODEL = 1024
BATCH = 8
SEQ = 2048
DEPTH = 1
DEC_BATCH = 128
DEC_SEQ = 8
PAST_LEN = 8192
PAGE_SIZE = 128

N_HEADS = 8
N_KV = 2
REP = N_HEADS // N_KV
HEAD_DIM = 64
Q_WIDTH = N_HEADS * HEAD_DIM
KV_WIDTH = N_KV * HEAD_DIM
WINDOW = 128
ATT_BLOCK = 128
W_BUF = min(WINDOW, PAST_LEN)
N_BUCKETS = 32
MAX_DISTANCE = 128
C_GROUPS = 4
C_CHUNK = 128
C_WIDTH = 512
C_GDIM = C_WIDTH // C_GROUPS
PLE_DIM = 256
PEER_HEADS = 8
N_KEYS = 128
N_EXPERTS = N_KEYS * N_KEYS
PEER_TOPK = 16
PEER_DKEY = 256
PEER_DHALF = PEER_DKEY // 2
PEER_BLOCK = 128
IN_WIDTH = Q_WIDTH + 2 * KV_WIDTH + 2 * C_WIDTH + 2 * D_MODEL
ALPHA = (2.0 * DEPTH) ** 0.25
BETA = (8.0 * DEPTH) ** -0.25
LN_EPS = 1e-5
NEG = -1e30

kernel_name = 'hybrid_swa_sink_gmlp_peer_step'


def layer_norm(x, g, b):
    xf = x.astype(jnp.float32)
    mu = xf.mean(-1, keepdims=True)
    var = jnp.mean(jnp.square(xf - mu), -1, keepdims=True)
    return ((xf - mu) * lax.rsqrt(var + LN_EPS)).astype(x.dtype) * g + b


def split_in(z):
    idx = np.cumsum([Q_WIDTH, KV_WIDTH, KV_WIDTH, C_WIDTH, C_WIDTH, D_MODEL]).tolist()
    return jnp.split(z, idx, axis=-1)


def t5_bias(dist, table):
    n = jnp.maximum(dist, 0)
    max_exact = N_BUCKETS // 2
    nf = jnp.maximum(n, 1).astype(jnp.float32)
    large = max_exact + (jnp.log(nf / max_exact) / math.log(MAX_DISTANCE / max_exact)
                         * (N_BUCKETS - max_exact)).astype(jnp.int32)
    large = jnp.minimum(large, N_BUCKETS - 1)
    bucket = jnp.where(n < max_exact, n, large)
    return jnp.moveaxis(table[bucket], -1, 0)


def prompt_band_pattern(table):
    nb = SEQ // ATT_BLOCK
    qi = jnp.arange(ATT_BLOCK)[:, None]
    kj = jnp.arange(2 * ATT_BLOCK)[None, :]
    dist = qi - kj + ATT_BLOCK
    kpos = jnp.arange(nb)[:, None, None] * ATT_BLOCK - ATT_BLOCK + kj[None]
    mask = (dist >= 0) & (dist < WINDOW) & (kpos >= 0)
    return t5_bias(dist, table), mask


def sample_band_pattern(table, n_new):
    qi = jnp.arange(n_new)[:, None]
    kj = jnp.arange(W_BUF + n_new)[None, :]
    dist = qi - (kj - W_BUF)
    mask = (dist >= 0) & (dist < WINDOW)
    return t5_bias(dist, table), mask


def banded_attention(q, k, v, mask, bias, sinks):
    s = jnp.einsum('bnqgrd,bnkgd->bngrqk', q, k).astype(jnp.float32) * (HEAD_DIM ** -0.5)
    s = s + bias.astype(jnp.float32).reshape(N_KV, REP, *bias.shape[1:])[None, None]
    s = jnp.where(mask[None, :, None, None], s, NEG)
    sink = sinks.astype(jnp.float32).reshape(N_KV, REP)[None, None, :, :, None]
    m = jnp.maximum(s.max(-1), sink)
    e = jnp.exp(s - m[..., None])
    w = e / (e.sum(-1) + jnp.exp(sink - m))[..., None]
    return jnp.einsum('bngrqk,bnkgd->bnqgrd', w.astype(v.dtype), v)


def shift_block(t):
    return jnp.pad(t, ((0, 0), (1, 0), (0, 0), (0, 0), (0, 0)))[:, :-1]


def attn_prompt(q, k, v, sinks, bias, mask):
    B, T = q.shape[:2]
    nb = T // ATT_BLOCK
    qb = q.reshape(B, nb, ATT_BLOCK, N_KV, REP, HEAD_DIM)
    kb = k.reshape(B, nb, ATT_BLOCK, N_KV, HEAD_DIM)
    vb = v.reshape(B, nb, ATT_BLOCK, N_KV, HEAD_DIM)
    kk = jnp.concatenate([shift_block(kb), kb], axis=2)
    vv = jnp.concatenate([shift_block(vb), vb], axis=2)
    o = banded_attention(qb, kk, vv, mask, bias, sinks)
    return o.reshape(B, T, Q_WIDTH)


def attn_sample(q, k, v, k_cache, v_cache, sinks, bias, mask):
    Bd, S = q.shape[:2]
    k_all = jnp.concatenate([k_cache, k.reshape(Bd, S, N_KV, HEAD_DIM)], axis=1)
    v_all = jnp.concatenate([v_cache, v.reshape(Bd, S, N_KV, HEAD_DIM)], axis=1)
    o = banded_attention(q.reshape(Bd, 1, S, N_KV, REP, HEAD_DIM), k_all[:, None], v_all[:, None],
                         mask[None], bias, sinks)
    return o.reshape(Bd, S, Q_WIDTH), k_all[:, -W_BUF:], v_all[:, -W_BUF:]


def chunk_prompt(u, v, ws, bs):
    B, T = v.shape[:2]
    nc = T // C_CHUNK
    vb = v.reshape(B, nc, C_CHUNK, C_GROUPS, C_GDIM)
    wm = ws * jnp.tril(jnp.ones((C_CHUNK, C_CHUNK), ws.dtype))
    s = jnp.einsum('gts,bcsgd->bctgd', wm, vb) + bs.T[:, :, None]
    return u * s.reshape(B, T, C_WIDTH)


def chunk_sample(u, v, ws, bs):
    Bd, S = v.shape[:2]
    wm = (ws * jnp.tril(jnp.ones((C_CHUNK, C_CHUNK), ws.dtype)))[:, :S, :S]
    s = jnp.einsum('gts,bsgd->btgd', wm, v.reshape(Bd, S, C_GROUPS, C_GDIM)) + bs[:, :S].T[:, :, None]
    return u * s.reshape(Bd, S, C_WIDTH)


def peer_ffn(x, wq, subkeys, u_tab, v_tab):
    shp = x.shape
    xf = x.reshape(-1, D_MODEL)
    T = xf.shape[0]
    nblk = -(-T // PEER_BLOCK)
    xp = jnp.pad(xf, ((0, nblk * PEER_BLOCK - T), (0, 0))).reshape(nblk, PEER_BLOCK, D_MODEL)

    def block(xb):
        q = (xb @ wq).reshape(PEER_BLOCK, PEER_HEADS, 2, PEER_DHALF)
        s = jnp.einsum('thcd,hcnd->thcn', q, subkeys).astype(jnp.float32)
        sv, si = lax.top_k(s, PEER_TOPK)
        cand = (sv[:, :, 0, :, None] + sv[:, :, 1, None, :]).reshape(PEER_BLOCK, PEER_HEADS, PEER_TOPK * PEER_TOPK)
        fv, fi = lax.top_k(cand, PEER_TOPK)
        i1 = jnp.take_along_axis(si[:, :, 0], fi // PEER_TOPK, axis=-1)
        i2 = jnp.take_along_axis(si[:, :, 1], fi % PEER_TOPK, axis=-1)
        e = i1 * N_KEYS + i2
        g = jax.nn.softmax(fv, axis=-1)
        a = jax.nn.gelu(jnp.einsum('thkd,td->thk', u_tab[e], xb))
        return jnp.einsum('thk,thkd->td', (g * a).astype(v_tab.dtype), v_tab[e])

    y = lax.map(block, xp).reshape(-1, D_MODEL)[:T]
    return y.reshape(shp)


def finish_layer(x, p, ya, yc, ga, gc, wa, wc, wo, l1g, l1b, pwq, psk, pu, pv, wpg, wpp, l2g, l2b):
    mix = jax.nn.sigmoid(ga) * (ya @ wa) + jax.nn.sigmoid(gc) * (yc @ wc)
    x1 = layer_norm(ALPHA * x + mix @ wo, l1g, l1b)
    f = peer_ffn(x1, pwq, psk, pu, pv)
    e = jax.nn.sigmoid(x1 @ wpg) * (p @ wpp)
    return layer_norm(ALPHA * x1 + f + e, l2g, l2b)


def setup_inputs(seed: int = 0) -> dict:
    key = jax.random.key(seed)
    ks = list(jax.random.split(key, 28))

    def nrm(shape, scale):
        return jax.random.normal(ks.pop(), shape, jnp.float32) * scale

    L = DEPTH
    return {
        'x_prompt': nrm((BATCH, SEQ, D_MODEL), 1.0),
        'x_sample': nrm((DEC_BATCH, DEC_SEQ, D_MODEL), 1.0),
        'cache_k_win': nrm((L, DEC_BATCH, W_BUF, N_KV, HEAD_DIM), 1.0),
        'cache_v_win': nrm((L, DEC_BATCH, W_BUF, N_KV, HEAD_DIM), 1.0),
        'p_prompt': nrm((L, BATCH, SEQ, PLE_DIM), 1.0),
        'p_sample': nrm((L, DEC_BATCH, DEC_SEQ, PLE_DIM), 1.0),
        'rel_bias_table': nrm((N_BUCKETS, N_HEADS), 0.5),
        'w_in': nrm((L, D_MODEL, IN_WIDTH), D_MODEL ** -0.5),
        'attn_sinks': nrm((L, N_HEADS), 1.0),
        'w_att_out': nrm((L, Q_WIDTH, D_MODEL), Q_WIDTH ** -0.5),
        'c_ln_g': 1.0 + nrm((L, C_WIDTH), 0.05),
        'c_ln_b': nrm((L, C_WIDTH), 0.05),
        'c_ws': nrm((L, C_GROUPS, C_CHUNK, C_CHUNK), C_CHUNK ** -0.5),
        'c_bs': 1.0 + nrm((L, C_GROUPS, C_CHUNK), 0.1),
        'w_chunk_out': nrm((L, C_WIDTH, D_MODEL), C_WIDTH ** -0.5),
        'w_o': nrm((L, D_MODEL, D_MODEL), BETA * D_MODEL ** -0.5),
        'ln1_g': 1.0 + nrm((L, D_MODEL), 0.05),
        'ln1_b': nrm((L, D_MODEL), 0.05),
        'peer_wq': nrm((L, D_MODEL, PEER_HEADS * PEER_DKEY), D_MODEL ** -0.5),
        'peer_subkeys': nrm((L, PEER_HEADS, 2, N_KEYS, PEER_DHALF), PEER_DHALF ** -0.5),
        'peer_u': nrm((L, N_EXPERTS, D_MODEL), D_MODEL ** -0.5),
        'peer_v': nrm((L, N_EXPERTS, D_MODEL), BETA),
        'w_ple_gate': nrm((L, D_MODEL, D_MODEL), D_MODEL ** -0.5),
        'w_ple_proj': nrm((L, PLE_DIM, D_MODEL), BETA * PLE_DIM ** -0.5),
        'ln2_g': 1.0 + nrm((L, D_MODEL), 0.05),
        'ln2_b': nrm((L, D_MODEL), 0.05),
    }


def reference(x_prompt, x_sample, cache_k_win, cache_v_win, p_prompt, p_sample, rel_bias_table,
              w_in, attn_sinks, w_att_out, c_ln_g, c_ln_b, c_ws, c_bs, w_chunk_out, w_o,
              ln1_g, ln1_b, peer_wq, peer_subkeys, peer_u, peer_v, w_ple_gate, w_ple_proj,
              ln2_g, ln2_b):
    bias_p, mask_p = prompt_band_pattern(rel_bias_table)
    bias_s, mask_s = sample_band_pattern(rel_bias_table, x_sample.shape[1])
    xp, xs = x_prompt, x_sample
    kp_l, vp_l, ks_l, vs_l, cs_l = [], [], [], [], []
    for i in range(DEPTH):
        shared = (w_att_out[i], w_chunk_out[i], w_o[i], ln1_g[i], ln1_b[i], peer_wq[i],
                  peer_subkeys[i], peer_u[i], peer_v[i], w_ple_gate[i], w_ple_proj[i],
                  ln2_g[i], ln2_b[i])
        q, k, v, cu, cv, ga, gc = split_in(xp @ w_in[i])
        cv = layer_norm(cv, c_ln_g[i], c_ln_b[i])
        ya = attn_prompt(q, k, v, attn_sinks[i], bias_p, mask_p)
        yc = chunk_prompt(cu, cv, c_ws[i], c_bs[i])
        kp_l.append(k.reshape(k.shape[0], k.shape[1], N_KV, HEAD_DIM)[:, -W_BUF:])
        vp_l.append(v.reshape(v.shape[0], v.shape[1], N_KV, HEAD_DIM)[:, -W_BUF:])
        xp = finish_layer(xp, p_prompt[i], ya, yc, ga, gc, *shared)
        q, k, v, cu, cv, ga, gc = split_in(xs @ w_in[i])
        cv = layer_norm(cv, c_ln_g[i], c_ln_b[i])
        ya, k_new, v_new = attn_sample(q, k, v, cache_k_win[i], cache_v_win[i], attn_sinks[i], bias_s, mask_s)
        yc = chunk_sample(cu, cv, c_ws[i], c_bs[i])
        ks_l.append(k_new)
        vs_l.append(v_new)
        cs_l.append(cv)
        xs = finish_layer(xs, p_sample[i], ya, yc, ga, gc, *shared)
    return (xp, xs, jnp.stack(kp_l), jnp.stack(vp_l), jnp.stack(ks_l), jnp.stack(vs_l), jnp.stack(cs_l))
```

```python
import functools
import math

import jax
import jax.numpy as jnp
from jax import lax
from jax.experimental import pallas as pl
from jax.experimental.pallas import tpu as pltpu

F32 = jnp.float32
BF16 = jnp.bfloat16
I32 = jnp.int32

D_MODEL = 1024
N_HEADS = 8
N_KV = 2
REP = N_HEADS // N_KV
HEAD_DIM = 64
Q_WIDTH = N_HEADS * HEAD_DIM
KV_WIDTH = N_KV * HEAD_DIM
WINDOW = 128
ATT_BLOCK = 128
N_BUCKETS = 32
MAX_DISTANCE = 128
C_GROUPS = 4
C_CHUNK = 128
C_WIDTH = 512
C_GDIM = C_WIDTH // C_GROUPS
PLE_DIM = 256
PEER_HEADS = 8
N_KEYS = 128
N_EXPERTS = N_KEYS * N_KEYS
PEER_TOPK = 16
PEER_DKEY = 256
PEER_DHALF = PEER_DKEY // 2
PEER_PICKS = PEER_HEADS * PEER_TOPK
DEPTH = 1
ALPHA = (2.0 * DEPTH) ** 0.25
LN_EPS = 1e-5
NEG = -1e30
IN_WIDTH = Q_WIDTH + 2 * KV_WIDTH + 2 * C_WIDTH + 2 * D_MODEL

LANES = 128
SUBLANES = 8
MIB = 1024 * 1024

PEER_GROUP = SUBLANES
PEER_ROWS = PEER_GROUP * PEER_PICKS
PEER_PITCH = PEER_ROWS + SUBLANES
PACK_CHUNKS = D_MODEL // (2 * LANES)
PEER_TOKENS_PER_STEP = 64


def _layer_norm(x, g, b):
    mu = jnp.mean(x, axis=-1, keepdims=True)
    xc = x - mu
    var = jnp.mean(xc * xc, axis=-1, keepdims=True)
    return xc * lax.rsqrt(var + LN_EPS) * g + b


def _dot(a, b):
    return jnp.dot(a.astype(BF16), b.astype(BF16), preferred_element_type=F32)


def _dot_nt(a, b):
    return lax.dot_general(a.astype(BF16), b.astype(BF16), (((1,), (1,)), ((), ())),
                           preferred_element_type=F32)


def _inproj_kernel(x_ref, w_ref, g_ref, b_ref, q_ref, kv_ref, cu_ref, cvn_ref, gate_ref):
    z = jnp.dot(x_ref[...].astype(BF16), w_ref[...], preferred_element_type=F32)
    o = 0
    q_ref[...] = z[:, o:o + Q_WIDTH]
    o += Q_WIDTH
    kv_ref[...] = z[:, o:o + 2 * KV_WIDTH]
    o += 2 * KV_WIDTH
    cu_ref[...] = z[:, o:o + C_WIDTH]
    o += C_WIDTH
    cvn_ref[...] = _layer_norm(z[:, o:o + C_WIDTH], g_ref[...], b_ref[...])
    o += C_WIDTH
    gate_ref[...] = z[:, o:o + 2 * D_MODEL]


def _inproj(x, w_bf16, c_g, c_b, tm):
    T = x.shape[0]
    widths = (Q_WIDTH, 2 * KV_WIDTH, C_WIDTH, C_WIDTH, 2 * D_MODEL)
    row = lambda n: pl.BlockSpec((tm, n), lambda i: (i, 0))
    full = lambda a: pl.BlockSpec(a.shape, lambda i: (0,) * a.ndim)
    return pl.pallas_call(
        _inproj_kernel,
        grid=(T // tm,),
        in_specs=[row(D_MODEL), full(w_bf16), full(c_g), full(c_b)],
        out_specs=[row(n) for n in widths],
        out_shape=[jax.ShapeDtypeStruct((T, n), F32) for n in widths],
        compiler_params=pltpu.CompilerParams(
            dimension_semantics=("arbitrary",), vmem_limit_bytes=48 * MIB),
        name="inproj",
    )(x, w_bf16, c_g, c_b)


def _merge_and_norm(x, ya, yc, gate, wa_ref, wc_ref, wo_ref, g_ref, b_ref):
    ga = gate[:, :D_MODEL]
    gc = gate[:, D_MODEL:]
    mix = jax.nn.sigmoid(ga) * _dot(ya, wa_ref[...]) + jax.nn.sigmoid(gc) * _dot(yc, wc_ref[...])
    h = ALPHA * x + _dot(mix, wo_ref[...])
    return _layer_norm(h, g_ref[...], b_ref[...])


def _softmax_with_sink(s, sink):
    m = jnp.maximum(jnp.max(s, axis=-1, keepdims=True), sink)
    e = jnp.exp(s - m)
    return e / (jnp.sum(e, axis=-1, keepdims=True) + jnp.exp(sink - m))


def _prompt_block_kernel(nb, sinks_ref, q_ref, kvp_ref, kvo_ref, cu_ref, cvn_ref, gate_ref,
                         x_ref, bias_ref, ws_ref, bs_ref, wa_ref, wc_ref, wo_ref, g_ref,
                         b_ref, x1_ref):
    n = lax.rem(pl.program_id(0), nb)
    blk = ATT_BLOCK
    qi = lax.broadcasted_iota(I32, (blk, 2 * blk), 0)
    kj = lax.broadcasted_iota(I32, (blk, 2 * blk), 1)
    dist = qi - kj + blk
    mask = (dist >= 0) & (dist < WINDOW) & ((kj >= blk) | (n > 0))

    kv = jnp.concatenate([kvp_ref[...], kvo_ref[...]], axis=0)
    q = q_ref[...]
    outs = []
    for h in range(N_HEADS):
        g = h // REP
        qh = q[:, h * HEAD_DIM:(h + 1) * HEAD_DIM]
        kg = kv[:, g * HEAD_DIM:(g + 1) * HEAD_DIM]
        vg = kv[:, KV_WIDTH + g * HEAD_DIM:KV_WIDTH + (g + 1) * HEAD_DIM]
        s = _dot_nt(qh, kg) * (HEAD_DIM ** -0.5) + bias_ref[h]
        s = jnp.where(mask, s, NEG)
        w = _softmax_with_sink(s, sinks_ref[h])
        outs.append(_dot(w, vg))
    ya = jnp.concatenate(outs, axis=1)

    ti = lax.broadcasted_iota(I32, (C_CHUNK, C_CHUNK), 0)
    si = lax.broadcasted_iota(I32, (C_CHUNK, C_CHUNK), 1)
    cvn = cvn_ref[...]
    parts = []
    for g in range(C_GROUPS):
        wm = jnp.where(si <= ti, ws_ref[g], 0.0)
        parts.append(_dot(wm, cvn[:, g * C_GDIM:(g + 1) * C_GDIM]))
    yc = cu_ref[...] * (jnp.concatenate(parts, axis=1) + bs_ref[...])

    x1_ref[...] = _merge_and_norm(x_ref[...], ya, yc, gate_ref[...], wa_ref, wc_ref, wo_ref,
                                  g_ref, b_ref)


def _prompt_blocks(n_tokens_total, nb, sinks, q, kv, cu, cvn, gate, x, bias, ws, bs_exp,
                   wa, wc, wo, ln_g, ln_b):
    T = q.shape[0]
    blk = ATT_BLOCK
    row = lambda n: pl.BlockSpec((blk, n), lambda i: (i, 0))
    full = lambda a: pl.BlockSpec(a.shape, lambda i: (0,) * a.ndim)
    prev = pl.BlockSpec((blk, 2 * KV_WIDTH), lambda i: (jnp.maximum(i - 1, 0), 0))
    return pl.pallas_call(
        functools.partial(_prompt_block_kernel, nb),
        grid=(T // blk,),
        in_specs=[pl.BlockSpec(memory_space=pltpu.SMEM),
                  row(Q_WIDTH), prev, row(2 * KV_WIDTH), row(C_WIDTH), row(C_WIDTH),
                  row(2 * D_MODEL), row(D_MODEL), full(bias), full(ws), full(bs_exp),
                  full(wa), full(wc), full(wo), full(ln_g), full(ln_b)],
        out_specs=row(D_MODEL),
        out_shape=jax.ShapeDtypeStruct((n_tokens_total, D_MODEL), F32),
        compiler_params=pltpu.CompilerParams(
            dimension_semantics=("arbitrary",), vmem_limit_bytes=48 * MIB),
        name="prompt_blocks",
    )(sinks, q, kv, kv, cu, cvn, gate, x, bias, ws, bs_exp, wa, wc, wo, ln_g, ln_b)


def _sample_block_kernel(seqs, s_new, sinks_ref, q_ref, kv_ref, cu_ref, cvn_ref, gate_ref,
                         x_ref, ck_ref, cv_ref, biasc_ref, biasn_ref, wbd_ref, bs_ref, wa_ref,
                         wc_ref, wo_ref, g_ref, b_ref, x1_alias_ref, x1_ref, kout_ref, vout_ref):
    del x1_alias_ref
    w_buf = ck_ref.shape[1]
    rows = REP * s_new
    qi_c = lax.rem(lax.broadcasted_iota(I32, (rows, w_buf), 0), s_new)
    kj_c = lax.broadcasted_iota(I32, (rows, w_buf), 1)
    mask_c = (qi_c + w_buf - kj_c) < WINDOW
    qi_n = lax.rem(lax.broadcasted_iota(I32, (rows, s_new), 0), s_new)
    kj_n = lax.broadcasted_iota(I32, (rows, s_new), 1)
    mask_n = kj_n <= qi_n
    sink_col = [jnp.concatenate([jnp.full((s_new, 1), sinks_ref[g * REP + r], F32)
                                 for r in range(REP)], axis=0) for g in range(N_KV)]

    q_all = q_ref[...]
    kv_all = kv_ref[...]
    ya_rows = []
    for b in range(seqs):
        qb = q_all[b * s_new:(b + 1) * s_new]
        kvb = kv_all[b * s_new:(b + 1) * s_new]
        ck = ck_ref[b]
        cv = cv_ref[b]
        kout_ref[b] = jnp.concatenate([ck[s_new:], kvb[:, :KV_WIDTH]], axis=0)
        vout_ref[b] = jnp.concatenate([cv[s_new:], kvb[:, KV_WIDTH:]], axis=0)
        heads = []
        for g in range(N_KV):
            qg = jnp.concatenate([qb[:, (g * REP + r) * HEAD_DIM:(g * REP + r + 1) * HEAD_DIM]
                                  for r in range(REP)], axis=0)
            lane = slice(g * HEAD_DIM, (g + 1) * HEAD_DIM)
            sc = _dot_nt(qg, ck[:, lane]) * (HEAD_DIM ** -0.5) + biasc_ref[g]
            sn = _dot_nt(qg, kvb[:, lane]) * (HEAD_DIM ** -0.5) + biasn_ref[g]
            sc = jnp.where(mask_c, sc, NEG)
            sn = jnp.where(mask_n, sn, NEG)
            sink = sink_col[g]
            m = jnp.maximum(jnp.maximum(jnp.max(sc, axis=-1, keepdims=True),
                                        jnp.max(sn, axis=-1, keepdims=True)), sink)
            ec = jnp.exp(sc - m)
            en = jnp.exp(sn - m)
            den = (jnp.sum(ec, axis=-1, keepdims=True) + jnp.sum(en, axis=-1, keepdims=True)
                   + jnp.exp(sink - m))
            vlane = slice(KV_WIDTH + g * HEAD_DIM, KV_WIDTH + (g + 1) * HEAD_DIM)
            og = _dot(ec / den, cv[:, lane]) + _dot(en / den, kvb[:, vlane])
            heads.extend(og[r * s_new:(r + 1) * s_new] for r in range(REP))
        ya_rows.append(jnp.concatenate(heads, axis=1))
    ya = jnp.concatenate(ya_rows, axis=0)

    cvn = cvn_ref[...]
    parts = [_dot(wbd_ref[g], cvn[:, g * C_GDIM:(g + 1) * C_GDIM]) for g in range(C_GROUPS)]
    yc = cu_ref[...] * (jnp.concatenate(parts, axis=1) + bs_ref[...])

    x1_ref[...] = _merge_and_norm(x_ref[...], ya, yc, gate_ref[...], wa_ref, wc_ref, wo_ref,
                                  g_ref, b_ref)


def _sample_blocks(x1_full, row_offset, seqs, s_new, sinks, q, kv, cu, cvn, gate, x, cache_k,
                   cache_v, bias_c, bias_n, wbd, bs_exp, wa, wc, wo, ln_g, ln_b):
    T = q.shape[0]
    n_seq, w_buf, kvw = cache_k.shape
    tm = seqs * s_new
    off = row_offset // tm
    row = lambda n: pl.BlockSpec((tm, n), lambda i: (i, 0))
    full = lambda a: pl.BlockSpec(a.shape, lambda i: (0,) * a.ndim)
    cache = pl.BlockSpec((seqs, w_buf, kvw), lambda i: (i, 0, 0))
    return pl.pallas_call(
        functools.partial(_sample_block_kernel, seqs, s_new),
        grid=(T // tm,),
        in_specs=[pl.BlockSpec(memory_space=pltpu.SMEM),
                  row(Q_WIDTH), row(2 * KV_WIDTH), row(C_WIDTH), row(C_WIDTH),
                  row(2 * D_MODEL), row(D_MODEL), cache, cache, full(bias_c), full(bias_n),
                  full(wbd), full(bs_exp), full(wa), full(wc), full(wo), full(ln_g),
                  full(ln_b), pl.BlockSpec(memory_space=pl.ANY)],
        out_specs=[pl.BlockSpec((tm, D_MODEL), lambda i: (i + off, 0)), cache, cache],
        out_shape=[jax.ShapeDtypeStruct(x1_full.shape, F32),
                   jax.ShapeDtypeStruct(cache_k.shape, F32),
                   jax.ShapeDtypeStruct(cache_v.shape, F32)],
        input_output_aliases={18: 0},
        compiler_params=pltpu.CompilerParams(
            dimension_semantics=("arbitrary",), vmem_limit_bytes=48 * MIB),
        name="sample_blocks",
    )(sinks, q, kv, cu, cvn, gate, x, cache_k, cache_v, bias_c, bias_n, wbd, bs_exp, wa, wc,
      wo, ln_g, ln_b, x1_full)


def _top16(s, iota, n_rows):
    vals, idxs = [], []
    for _ in range(PEER_TOPK):
        m = jnp.max(s, axis=0, keepdims=True)
        i = jnp.min(jnp.where(s == m, iota, n_rows), axis=0, keepdims=True)
        vals.append(m)
        idxs.append(i)
        s = jnp.where(iota == i, -jnp.inf, s)
    return jnp.concatenate(vals, axis=0), jnp.concatenate(idxs, axis=0)


def _pick(table, iota, idx):
    return jnp.concatenate(
        [jnp.sum(jnp.where(iota == idx[r:r + 1], table, 0), axis=0, keepdims=True)
         for r in range(PEER_TOPK)], axis=0)


def _route_kernel(x1_ref, wq_ref, sk_ref, e_ref, g_ref):
    tt = x1_ref.shape[0]
    x = x1_ref[...].astype(BF16)
    iota_n = lax.broadcasted_iota(I32, (N_KEYS, tt), 0)
    iota_c = lax.broadcasted_iota(I32, (PEER_TOPK * PEER_TOPK, tt), 0)
    iota_k = lax.broadcasted_iota(I32, (PEER_TOPK, tt), 0)

    def head(h, carry):
        qh = jnp.dot(x, wq_ref[h], preferred_element_type=F32)
        sv, si = [], []
        for c in range(2):
            s = _dot_nt(sk_ref[h, c], qh[:, c * PEER_DHALF:(c + 1) * PEER_DHALF])
            v, i = _top16(s, iota_n, N_KEYS)
            sv.append(v)
            si.append(i)
        cand = jnp.concatenate([sv[0][k:k + 1] + sv[1] for k in range(PEER_TOPK)], axis=0)
        fv, fi = _top16(cand, iota_c, PEER_TOPK * PEER_TOPK)
        i1 = _pick(si[0], iota_k, lax.shift_right_logical(fi, 4))
        i2 = _pick(si[1], iota_k, lax.bitwise_and(fi, PEER_TOPK - 1))
        ex = jnp.exp(fv - fv[0:1])
        row0 = pl.multiple_of(h * PEER_TOPK, PEER_TOPK)
        e_ref[0, pl.ds(row0, PEER_TOPK), :] = (i1 * N_KEYS + i2) * PACK_CHUNKS
        g_ref[0, pl.ds(row0, PEER_TOPK), :] = ex / jnp.sum(ex, axis=0, keepdims=True)
        return carry

    lax.fori_loop(0, PEER_HEADS, head, 0)


def _route(x1, wq_heads_bf16, subkeys, tt):
    T = x1.shape[0]
    full = lambda a: pl.BlockSpec(a.shape, lambda i: (0,) * a.ndim)
    out = pl.BlockSpec((1, PEER_PICKS, tt), lambda i: (i, 0, 0))
    return pl.pallas_call(
        _route_kernel,
        grid=(T // tt,),
        in_specs=[pl.BlockSpec((tt, D_MODEL), lambda i: (i, 0)), full(wq_heads_bf16),
                  full(subkeys)],
        out_specs=[out, out],
        out_shape=[jax.ShapeDtypeStruct((T // tt, PEER_PICKS, tt), I32),
                   jax.ShapeDtypeStruct((T // tt, PEER_PICKS, tt), F32)],
        compiler_params=pltpu.CompilerParams(
            dimension_semantics=("arbitrary",), vmem_limit_bytes=48 * MIB),
        name="peer_route",
    )(x1, wq_heads_bf16, subkeys)


def _pack_kernel(t_ref, o_ref):
    half = D_MODEL // 2
    lo = pltpu.bitcast(t_ref[:, :half].astype(BF16).astype(F32), jnp.uint32)
    hi = pltpu.bitcast(t_ref[:, half:].astype(BF16).astype(F32), jnp.uint32)
    word = (hi & jnp.uint32(0xFFFF0000)) | (lo >> 16)
    o_ref[...] = pltpu.bitcast(word, I32)


def _pack_table(tab, tm):
    n = tab.shape[0]
    packed = pl.pallas_call(
        _pack_kernel,
        grid=(n // tm,),
        in_specs=[pl.BlockSpec((tm, D_MODEL), lambda i: (i, 0))],
        out_specs=pl.BlockSpec((tm, D_MODEL // 2), lambda i: (i, 0)),
        out_shape=jax.ShapeDtypeStruct((n, D_MODEL // 2), I32),
        compiler_params=pltpu.CompilerParams(dimension_semantics=("arbitrary",)),
        name="pack_table",
    )(tab)
    return packed.reshape(n * PACK_CHUNKS, LANES)


def _unpack(word):
    lo = pltpu.bitcast(word << 16, F32)
    hi = pltpu.bitcast(word & jnp.int32(-65536), F32)
    return lo, hi


def _gather_group(e_ref, gi, tab_ref, tile_ref):
    for r in range(PEER_ROWS):
        row = pl.multiple_of(e_ref[gi, r], PACK_CHUNKS)
        tile_ref[pl.ds(r, PACK_CHUNKS, stride=PEER_PITCH), :] = tab_ref[pl.ds(row, PACK_CHUNKS), :]


def _tile_rows(tile_ref, c, j):
    return tile_ref[c * PEER_PITCH + j * PEER_GROUP:c * PEER_PITCH + (j + 1) * PEER_GROUP, :]


def _peer_u_kernel(e_ref, x_ref, g_ref, tab_ref, w_ref, tile_ref):
    lane = lax.broadcasted_iota(I32, (PEER_GROUP, PEER_PICKS), 1)

    def group(gi, carry):
        t0 = pl.multiple_of(gi * PEER_GROUP, PEER_GROUP)
        _gather_group(e_ref, gi, tab_ref, tile_ref)
        xg = x_ref[pl.ds(t0, PEER_GROUP), :]
        a = jnp.zeros((PEER_GROUP, PEER_PICKS), F32)
        for j in range(PEER_PICKS):
            p = None
            for c in range(PACK_CHUNKS):
                lo, hi = _unpack(_tile_rows(tile_ref, c, j))
                term = (lo * xg[:, c * LANES:(c + 1) * LANES]
                        + hi * xg[:, D_MODEL // 2 + c * LANES:D_MODEL // 2 + (c + 1) * LANES])
                p = term if p is None else p + term
            a = jnp.where(lane == j, jnp.sum(p, axis=1, keepdims=True), a)
        w_ref[pl.ds(t0, PEER_GROUP), :] = g_ref[pl.ds(t0, PEER_GROUP), :] * jax.nn.gelu(a)
        return carry

    lax.fori_loop(0, x_ref.shape[0] // PEER_GROUP, group, 0)


def _peer_v_kernel(e_ref, w_ref, tab_ref, f_ref, tile_ref):
    def group(gi, carry):
        t0 = pl.multiple_of(gi * PEER_GROUP, PEER_GROUP)
        _gather_group(e_ref, gi, tab_ref, tile_ref)
        wg = w_ref[pl.ds(t0, PEER_GROUP), :]
        acc = [jnp.zeros((PEER_GROUP, LANES), F32) for _ in range(2 * PACK_CHUNKS)]
        for j in range(PEER_PICKS):
            wb = jnp.broadcast_to(wg[:, j:j + 1], (PEER_GROUP, LANES))
            for c in range(PACK_CHUNKS):
                lo, hi = _unpack(_tile_rows(tile_ref, c, j))
                acc[c] = acc[c] + wb * lo
                acc[PACK_CHUNKS + c] = acc[PACK_CHUNKS + c] + wb * hi
        f_ref[pl.ds(t0, PEER_GROUP), :] = jnp.concatenate(acc, axis=1)
        return carry

    lax.fori_loop(0, w_ref.shape[0] // PEER_GROUP, group, 0)


def _peer_specs(tb):
    idx = pl.BlockSpec((tb // PEER_GROUP, PEER_ROWS), lambda i: (i, 0), memory_space=pltpu.SMEM)
    picks = pl.BlockSpec((tb, PEER_PICKS), lambda i: (i, 0))
    feat = pl.BlockSpec((tb, D_MODEL), lambda i: (i, 0))
    table = pl.BlockSpec((N_EXPERTS * PACK_CHUNKS, LANES), lambda i: (0, 0),
                         pipeline_mode=pl.Buffered(1))
    tile = pltpu.VMEM((PACK_CHUNKS * PEER_PITCH, LANES), I32)
    params = pltpu.CompilerParams(dimension_semantics=("arbitrary",),
                                  vmem_limit_bytes=48 * MIB)
    return idx, picks, feat, table, tile, params


def _peer_u(e_grp, x1, g, tab_u, tb):
    T = x1.shape[0]
    idx, picks, feat, table, tile, params = _peer_specs(tb)
    return pl.pallas_call(
        _peer_u_kernel, grid=(T // tb,),
        in_specs=[idx, feat, picks, table], out_specs=picks,
        out_shape=jax.ShapeDtypeStruct((T, PEER_PICKS), F32),
        scratch_shapes=[tile], compiler_params=params, name="peer_u",
    )(e_grp, x1, g, tab_u)


def _peer_v(e_grp, w, tab_v, tb):
    T = w.shape[0]
    idx, picks, feat, table, tile, params = _peer_specs(tb)
    return pl.pallas_call(
        _peer_v_kernel, grid=(T // tb,),
        in_specs=[idx, picks, table], out_specs=feat,
        out_shape=jax.ShapeDtypeStruct((T, D_MODEL), F32),
        scratch_shapes=[tile], compiler_params=params, name="peer_v",
    )(e_grp, w, tab_v)


def _final_kernel(x1_ref, f_ref, p_ref, wg_ref, wp_ref, g_ref, b_ref, y_ref):
    x1 = x1_ref[...]
    e = jax.nn.sigmoid(_dot(x1, wg_ref[...])) * _dot(p_ref[...], wp_ref[...])
    y_ref[...] = _layer_norm(ALPHA * x1 + f_ref[...] + e, g_ref[...], b_ref[...])


def _final(x1, f, p, row_offset, wg, wp, ln_g, ln_b, tm):
    T = p.shape[0]
    off = row_offset // tm
    full = lambda a: pl.BlockSpec(a.shape, lambda i: (0,) * a.ndim)
    shifted = pl.BlockSpec((tm, D_MODEL), lambda i: (i + off, 0))
    return pl.pallas_call(
        _final_kernel,
        grid=(T // tm,),
        in_specs=[shifted, shifted, pl.BlockSpec((tm, PLE_DIM), lambda i: (i, 0)),
                  full(wg), full(wp), full(ln_g), full(ln_b)],
        out_specs=pl.BlockSpec((tm, D_MODEL), lambda i: (i, 0)),
        out_shape=jax.ShapeDtypeStruct((T, D_MODEL), F32),
        compiler_params=pltpu.CompilerParams(
            dimension_semantics=("arbitrary",), vmem_limit_bytes=48 * MIB),
        name="final",
    )(x1, f, p, wg, wp, ln_g, ln_b)


def _t5_bias(dist, table):
    n = jnp.maximum(dist, 0)
    max_exact = N_BUCKETS // 2
    nf = jnp.maximum(n, 1).astype(F32)
    large = max_exact + (jnp.log(nf / max_exact) / math.log(MAX_DISTANCE / max_exact)
                         * (N_BUCKETS - max_exact)).astype(I32)
    large = jnp.minimum(large, N_BUCKETS - 1)
    bucket = jnp.where(n < max_exact, n, large)
    return jnp.moveaxis(table[bucket], -1, 0)


def kernel(x_prompt, x_sample, cache_k_win, cache_v_win, p_prompt, p_sample, rel_bias_table,
           w_in, attn_sinks, w_att_out, c_ln_g, c_ln_b, c_ws, c_bs, w_chunk_out, w_o, ln1_g,
           ln1_b, peer_wq, peer_subkeys, peer_u, peer_v, w_ple_gate, w_ple_proj, ln2_g, ln2_b):
    batch, seq, d = x_prompt.shape
    n_seq, s_new, _ = x_sample.shape
    w_buf = cache_k_win.shape[2]
    assert d == D_MODEL and w_in.shape[0] == DEPTH and seq % ATT_BLOCK == 0
    tp = batch * seq
    ts = n_seq * s_new
    t_all = tp + ts
    nb = seq // ATT_BLOCK
    seqs_per_step = 8
    assert n_seq % seqs_per_step == 0 and tp % (seqs_per_step * s_new) == 0
    assert t_all % LANES == 0 and tp % 256 == 0 and ts % 256 == 0

    row2 = lambda v: v.reshape(1, -1)
    w_in_b = w_in[0].astype(BF16)
    wa, wc, wo = w_att_out[0].astype(BF16), w_chunk_out[0].astype(BF16), w_o[0].astype(BF16)
    sinks = attn_sinks[0]
    cg, cb = row2(c_ln_g[0]), row2(c_ln_b[0])
    l1g, l1b = row2(ln1_g[0]), row2(ln1_b[0])
    l2g, l2b = row2(ln2_g[0]), row2(ln2_b[0])
    ws, bs = c_ws[0], c_bs[0]

    qi = jnp.arange(ATT_BLOCK)[:, None]
    kj = jnp.arange(2 * ATT_BLOCK)[None, :]
    bias_p = _t5_bias(qi - kj + ATT_BLOCK, rel_bias_table).astype(F32)
    qs = jnp.arange(s_new)[:, None]
    ks = jnp.arange(w_buf + s_new)[None, :]
    bias_s = _t5_bias(qs - (ks - w_buf), rel_bias_table).astype(F32)
    bias_s = bias_s.reshape(N_KV, REP * s_new, w_buf + s_new)
    bias_sc, bias_sn = bias_s[:, :, :w_buf], bias_s[:, :, w_buf:]

    bs_exp_p = jnp.repeat(bs.T, C_GDIM, axis=1)
    bs_exp_s = jnp.tile(jnp.repeat(bs[:, :s_new].T, C_GDIM, axis=1), (seqs_per_step, 1))
    tril = jnp.tril(jnp.ones((s_new, s_new), F32))
    eye = jnp.eye(seqs_per_step, dtype=F32)
    wbd = jnp.stack([jnp.kron(eye, ws[g, :s_new, :s_new] * tril) for g in range(C_GROUPS)])

    xp = x_prompt.reshape(tp, d)
    q, kv, cu, cvn, gate = _inproj(xp, w_in_b, cg, cb, 256)
    x1 = _prompt_blocks(t_all, nb, sinks, q, kv, cu, cvn, gate, xp, bias_p, ws, bs_exp_p,
                        wa, wc, wo, l1g, l1b)
    kv_tail = kv.reshape(batch, seq, 2 * KV_WIDTH)[:, seq - w_buf:]
    kp = kv_tail[..., :KV_WIDTH].reshape(1, batch, w_buf, N_KV, HEAD_DIM)
    vp = kv_tail[..., KV_WIDTH:].reshape(1, batch, w_buf, N_KV, HEAD_DIM)

    xs = x_sample.reshape(ts, d)
    q, kv, cu, cvn_s, gate = _inproj(xs, w_in_b, cg, cb, 256)
    x1, k_new, v_new = _sample_blocks(
        x1, tp, seqs_per_step, s_new, sinks, q, kv, cu, cvn_s, gate, xs,
        cache_k_win[0].reshape(n_seq, w_buf, KV_WIDTH),
        cache_v_win[0].reshape(n_seq, w_buf, KV_WIDTH),
        bias_sc, bias_sn, wbd, bs_exp_s, wa, wc, wo, l1g, l1b)
    ks_out = k_new.reshape(1, n_seq, w_buf, N_KV, HEAD_DIM)
    vs_out = v_new.reshape(1, n_seq, w_buf, N_KV, HEAD_DIM)
    cs_out = cvn_s.reshape(1, n_seq, s_new, C_WIDTH)

    wq_heads = peer_wq[0].reshape(D_MODEL, PEER_HEADS, PEER_DKEY).transpose(1, 0, 2).astype(BF16)
    e_t, g_t = _route(x1, wq_heads, peer_subkeys[0], LANES)
    nt = t_all // LANES
    e_grp = (e_t.reshape(nt, PEER_PICKS, LANES // PEER_GROUP, PEER_GROUP)
             .transpose(0, 2, 1, 3).reshape(t_all // PEER_GROUP, PEER_ROWS))
    g_tok = g_t.transpose(0, 2, 1).reshape(t_all, PEER_PICKS)
    tab_u = _pack_table(peer_u[0], 256)
    tab_v = _pack_table(peer_v[0], 256)
    w_tok = _peer_u(e_grp, x1, g_tok, tab_u, PEER_TOKENS_PER_STEP)
    f = _peer_v(e_grp, w_tok, tab_v, PEER_TOKENS_PER_STEP)

    wg, wp = w_ple_gate[0].astype(BF16), w_ple_proj[0].astype(BF16)
    yp = _final(x1, f, p_prompt[0].reshape(tp, PLE_DIM), 0, wg, wp, l2g, l2b, 256)
    ys = _final(x1, f, p_sample[0].reshape(ts, PLE_DIM), tp, wg, wp, l2g, l2b, 256)
    return (yp.reshape(batch, seq, d), ys.reshape(n_seq, s_new, d), kp, vp, ks_out, vs_out,
            cs_out)
```

```python
import functools
import math

import jax
import jax.numpy as jnp
from jax import lax
from jax.experimental import pallas as pl
from jax.experimental.pallas import tpu as pltpu

F32 = jnp.float32
BF16 = jnp.bfloat16
I32 = jnp.int32

D_MODEL = 1024
N_HEADS = 8
N_KV = 2
REP = N_HEADS // N_KV
HEAD_DIM = 64
Q_WIDTH = N_HEADS * HEAD_DIM
KV_WIDTH = N_KV * HEAD_DIM
WINDOW = 128
ATT_BLOCK = 128
N_BUCKETS = 32
MAX_DISTANCE = 128
C_GROUPS = 4
C_CHUNK = 128
C_WIDTH = 512
C_GDIM = C_WIDTH // C_GROUPS
PLE_DIM = 256
PEER_HEADS = 8
N_KEYS = 128
N_EXPERTS = N_KEYS * N_KEYS
PEER_TOPK = 16
PEER_DKEY = 256
PEER_DHALF = PEER_DKEY // 2
PEER_PICKS = PEER_HEADS * PEER_TOPK
DEPTH = 1
ALPHA = (2.0 * DEPTH) ** 0.25
LN_EPS = 1e-5
NEG = -1e30
IN_WIDTH = Q_WIDTH + 2 * KV_WIDTH + 2 * C_WIDTH + 2 * D_MODEL

LANES = 128
SUBLANES = 8
MIB = 1024 * 1024

PEER_GROUP = SUBLANES
PEER_ROWS = PEER_GROUP * PEER_PICKS
PEER_PITCH = PEER_ROWS + SUBLANES
PACK_CHUNKS = D_MODEL // (2 * LANES)
PEER_TOKENS_PER_STEP = 128


def _layer_norm(x, g, b):
    mu = jnp.mean(x, axis=-1, keepdims=True)
    xc = x - mu
    var = jnp.mean(xc * xc, axis=-1, keepdims=True)
    return xc * lax.rsqrt(var + LN_EPS) * g + b


def _dot(a, b):
    return jnp.dot(a.astype(BF16), b.astype(BF16), preferred_element_type=F32)


def _dot_nt(a, b):
    return lax.dot_general(a.astype(BF16), b.astype(BF16), (((1,), (1,)), ((), ())),
                           preferred_element_type=F32)


def _inproj_kernel(x_ref, w_ref, g_ref, b_ref, q_ref, kv_ref, cu_ref, cvn_ref, gate_ref):
    z = jnp.dot(x_ref[...].astype(BF16), w_ref[...], preferred_element_type=F32)
    o = 0
    q_ref[...] = z[:, o:o + Q_WIDTH]
    o += Q_WIDTH
    kv_ref[...] = z[:, o:o + 2 * KV_WIDTH]
    o += 2 * KV_WIDTH
    cu_ref[...] = z[:, o:o + C_WIDTH]
    o += C_WIDTH
    cvn_ref[...] = _layer_norm(z[:, o:o + C_WIDTH], g_ref[...], b_ref[...])
    o += C_WIDTH
    gate_ref[...] = z[:, o:o + 2 * D_MODEL]


def _inproj(x, w_bf16, c_g, c_b, tm):
    T = x.shape[0]
    widths = (Q_WIDTH, 2 * KV_WIDTH, C_WIDTH, C_WIDTH, 2 * D_MODEL)
    row = lambda n: pl.BlockSpec((tm, n), lambda i: (i, 0))
    full = lambda a: pl.BlockSpec(a.shape, lambda i: (0,) * a.ndim)
    return pl.pallas_call(
        _inproj_kernel,
        grid=(T // tm,),
        in_specs=[row(D_MODEL), full(w_bf16), full(c_g), full(c_b)],
        out_specs=[row(n) for n in widths],
        out_shape=[jax.ShapeDtypeStruct((T, n), F32) for n in widths],
        compiler_params=pltpu.CompilerParams(
            dimension_semantics=("arbitrary",), vmem_limit_bytes=48 * MIB),
        name="inproj",
    )(x, w_bf16, c_g, c_b)


def _merge_and_norm(x, ya, yc, gate, wa_ref, wc_ref, wo_ref, g_ref, b_ref):
    ga = gate[:, :D_MODEL]
    gc = gate[:, D_MODEL:]
    mix = jax.nn.sigmoid(ga) * _dot(ya, wa_ref[...]) + jax.nn.sigmoid(gc) * _dot(yc, wc_ref[...])
    h = ALPHA * x + _dot(mix, wo_ref[...])
    return _layer_norm(h, g_ref[...], b_ref[...])


def _softmax_with_sink(s, sink):
    m = jnp.maximum(jnp.max(s, axis=-1, keepdims=True), sink)
    e = jnp.exp(s - m)
    return e / (jnp.sum(e, axis=-1, keepdims=True) + jnp.exp(sink - m))


def _prompt_block_kernel(nb, sinks_ref, q_ref, kvp_ref, kvo_ref, cu_ref, cvn_ref, gate_ref,
                         x_ref, bias_ref, ws_ref, bs_ref, wa_ref, wc_ref, wo_ref, g_ref,
                         b_ref, x1_ref):
    n = lax.rem(pl.program_id(0), nb)
    blk = ATT_BLOCK
    qi = lax.broadcasted_iota(I32, (blk, 2 * blk), 0)
    kj = lax.broadcasted_iota(I32, (blk, 2 * blk), 1)
    dist = qi - kj + blk
    mask = (dist >= 0) & (dist < WINDOW) & ((kj >= blk) | (n > 0))

    kv = jnp.concatenate([kvp_ref[...], kvo_ref[...]], axis=0)
    q = q_ref[...]
    outs = []
    for h in range(N_HEADS):
        g = h // REP
        qh = q[:, h * HEAD_DIM:(h + 1) * HEAD_DIM]
        kg = kv[:, g * HEAD_DIM:(g + 1) * HEAD_DIM]
        vg = kv[:, KV_WIDTH + g * HEAD_DIM:KV_WIDTH + (g + 1) * HEAD_DIM]
        s = _dot_nt(qh, kg) * (HEAD_DIM ** -0.5) + bias_ref[h]
        s = jnp.where(mask, s, NEG)
        w = _softmax_with_sink(s, sinks_ref[h])
        outs.append(_dot(w, vg))
    ya = jnp.concatenate(outs, axis=1)

    ti = lax.broadcasted_iota(I32, (C_CHUNK, C_CHUNK), 0)
    si = lax.broadcasted_iota(I32, (C_CHUNK, C_CHUNK), 1)
    cvn = cvn_ref[...]
    parts = []
    for g in range(C_GROUPS):
        wm = jnp.where(si <= ti, ws_ref[g], 0.0)
        parts.append(_dot(wm, cvn[:, g * C_GDIM:(g + 1) * C_GDIM]))
    yc = cu_ref[...] * (jnp.concatenate(parts, axis=1) + bs_ref[...])

    x1_ref[...] = _merge_and_norm(x_ref[...], ya, yc, gate_ref[...], wa_ref, wc_ref, wo_ref,
                                  g_ref, b_ref)


def _prompt_blocks(nb, sinks, q, kv, cu, cvn, gate, x, bias, ws, bs_exp,
                   wa, wc, wo, ln_g, ln_b):
    T = q.shape[0]
    blk = ATT_BLOCK
    row = lambda n: pl.BlockSpec((blk, n), lambda i: (i, 0))
    full = lambda a: pl.BlockSpec(a.shape, lambda i: (0,) * a.ndim)
    prev = pl.BlockSpec((blk, 2 * KV_WIDTH), lambda i: (jnp.maximum(i - 1, 0), 0))
    return pl.pallas_call(
        functools.partial(_prompt_block_kernel, nb),
        grid=(T // blk,),
        in_specs=[pl.BlockSpec(memory_space=pltpu.SMEM),
                  row(Q_WIDTH), prev, row(2 * KV_WIDTH), row(C_WIDTH), row(C_WIDTH),
                  row(2 * D_MODEL), row(D_MODEL), full(bias), full(ws), full(bs_exp),
                  full(wa), full(wc), full(wo), full(ln_g), full(ln_b)],
        out_specs=row(D_MODEL),
        out_shape=jax.ShapeDtypeStruct((T, D_MODEL), F32),
        compiler_params=pltpu.CompilerParams(
            dimension_semantics=("arbitrary",), vmem_limit_bytes=48 * MIB),
        name="prompt_blocks",
    )(sinks, q, kv, kv, cu, cvn, gate, x, bias, ws, bs_exp, wa, wc, wo, ln_g, ln_b)


def _sample_block_kernel(seqs, s_new, sinks_ref, q_ref, kv_ref, cu_ref, cvn_ref, gate_ref,
                         x_ref, ck_ref, cv_ref, biasc_ref, biasn_ref, wbd_ref, bs_ref, wa_ref,
                         wc_ref, wo_ref, g_ref, b_ref, x1_ref, kout_ref, vout_ref):
    w_buf = ck_ref.shape[1]
    rows = REP * s_new
    qi_c = lax.rem(lax.broadcasted_iota(I32, (rows, w_buf), 0), s_new)
    kj_c = lax.broadcasted_iota(I32, (rows, w_buf), 1)
    mask_c = (qi_c + w_buf - kj_c) < WINDOW
    qi_n = lax.rem(lax.broadcasted_iota(I32, (rows, s_new), 0), s_new)
    kj_n = lax.broadcasted_iota(I32, (rows, s_new), 1)
    mask_n = kj_n <= qi_n
    sink_col = [jnp.concatenate([jnp.full((s_new, 1), sinks_ref[g * REP + r], F32)
                                 for r in range(REP)], axis=0) for g in range(N_KV)]

    q_all = q_ref[...]
    kv_all = kv_ref[...]
    ya_rows = []
    for b in range(seqs):
        qb = q_all[b * s_new:(b + 1) * s_new]
        kvb = kv_all[b * s_new:(b + 1) * s_new]
        ck = ck_ref[b]
        cv = cv_ref[b]
        kout_ref[b] = jnp.concatenate([ck[s_new:], kvb[:, :KV_WIDTH]], axis=0)
        vout_ref[b] = jnp.concatenate([cv[s_new:], kvb[:, KV_WIDTH:]], axis=0)
        heads = []
        for g in range(N_KV):
            qg = jnp.concatenate([qb[:, (g * REP + r) * HEAD_DIM:(g * REP + r + 1) * HEAD_DIM]
                                  for r in range(REP)], axis=0)
            lane = slice(g * HEAD_DIM, (g + 1) * HEAD_DIM)
            sc = _dot_nt(qg, ck[:, lane]) * (HEAD_DIM ** -0.5) + biasc_ref[g]
            sn = _dot_nt(qg, kvb[:, lane]) * (HEAD_DIM ** -0.5) + biasn_ref[g]
            sc = jnp.where(mask_c, sc, NEG)
            sn = jnp.where(mask_n, sn, NEG)
            sink = sink_col[g]
            m = jnp.maximum(jnp.maximum(jnp.max(sc, axis=-1, keepdims=True),
                                        jnp.max(sn, axis=-1, keepdims=True)), sink)
            ec = jnp.exp(sc - m)
            en = jnp.exp(sn - m)
            den = (jnp.sum(ec, axis=-1, keepdims=True) + jnp.sum(en, axis=-1, keepdims=True)
                   + jnp.exp(sink - m))
            vlane = slice(KV_WIDTH + g * HEAD_DIM, KV_WIDTH + (g + 1) * HEAD_DIM)
            og = _dot(ec / den, cv[:, lane]) + _dot(en / den, kvb[:, vlane])
            heads.extend(og[r * s_new:(r + 1) * s_new] for r in range(REP))
        ya_rows.append(jnp.concatenate(heads, axis=1))
    ya = jnp.concatenate(ya_rows, axis=0)

    cvn = cvn_ref[...]
    parts = [_dot(wbd_ref[g], cvn[:, g * C_GDIM:(g + 1) * C_GDIM]) for g in range(C_GROUPS)]
    yc = cu_ref[...] * (jnp.concatenate(parts, axis=1) + bs_ref[...])

    x1_ref[...] = _merge_and_norm(x_ref[...], ya, yc, gate_ref[...], wa_ref, wc_ref, wo_ref,
                                  g_ref, b_ref)


def _sample_blocks(seqs, s_new, sinks, q, kv, cu, cvn, gate, x, cache_k,
                   cache_v, bias_c, bias_n, wbd, bs_exp, wa, wc, wo, ln_g, ln_b):
    T = q.shape[0]
    n_seq, w_buf, kvw = cache_k.shape
    tm = seqs * s_new
    row = lambda n: pl.BlockSpec((tm, n), lambda i: (i, 0))
    full = lambda a: pl.BlockSpec(a.shape, lambda i: (0,) * a.ndim)
    cache = pl.BlockSpec((seqs, w_buf, kvw), lambda i: (i, 0, 0))
    return pl.pallas_call(
        functools.partial(_sample_block_kernel, seqs, s_new),
        grid=(T // tm,),
        in_specs=[pl.BlockSpec(memory_space=pltpu.SMEM),
                  row(Q_WIDTH), row(2 * KV_WIDTH), row(C_WIDTH), row(C_WIDTH),
                  row(2 * D_MODEL), row(D_MODEL), cache, cache, full(bias_c), full(bias_n),
                  full(wbd), full(bs_exp), full(wa), full(wc), full(wo), full(ln_g),
                  full(ln_b)],
        out_specs=[row(D_MODEL), cache, cache],
        out_shape=[jax.ShapeDtypeStruct((T, D_MODEL), F32),
                   jax.ShapeDtypeStruct(cache_k.shape, F32),
                   jax.ShapeDtypeStruct(cache_v.shape, F32)],
        compiler_params=pltpu.CompilerParams(
            dimension_semantics=("arbitrary",), vmem_limit_bytes=48 * MIB),
        name="sample_blocks",
    )(sinks, q, kv, cu, cvn, gate, x, cache_k, cache_v, bias_c, bias_n, wbd, bs_exp, wa, wc,
      wo, ln_g, ln_b)


def _two_part_specs(tiles_a, tile_rows, width):
    first = pl.BlockSpec((tile_rows, width), lambda i: (jnp.minimum(i, tiles_a - 1), 0))
    second = pl.BlockSpec((tile_rows, width), lambda i: (jnp.maximum(i - tiles_a, 0), 0))
    return first, second


def _two_part_tile(tiles_a, a_ref, b_ref):
    return jnp.where(pl.program_id(0) < tiles_a, a_ref[...], b_ref[...])


def _top16(s, iota, fill):
    vals, idxs = [], []
    for _ in range(PEER_TOPK):
        m = jnp.max(s, axis=0, keepdims=True)
        i = jnp.min(jnp.where(s == m, iota, fill), axis=0, keepdims=True)
        vals.append(m)
        idxs.append(i)
        s = jnp.where(iota == i, -jnp.inf, s)
    return jnp.concatenate(vals, axis=0), jnp.concatenate(idxs, axis=0)


def _pick(table, iota, idx):
    return jnp.concatenate(
        [jnp.sum(jnp.where(iota == idx[r:r + 1], table, 0), axis=0, keepdims=True)
         for r in range(PEER_TOPK)], axis=0)


_CAND_SHORT = SUBLANES


def _cand_flat_index(tt):
    i16 = lax.broadcasted_iota(I32, (PEER_TOPK, tt), 0)
    i8 = lax.broadcasted_iota(I32, (_CAND_SHORT, tt), 0)
    pieces = [i16] + [k * PEER_TOPK + i8 for k in range(1, _CAND_SHORT)]
    pieces.append((_CAND_SHORT + i8) * PEER_TOPK)
    return jnp.concatenate(pieces, axis=0)


def _cand_values(sv0, sv1):
    pieces = [sv0[0:1] + sv1]
    pieces += [sv0[k:k + 1] + sv1[0:_CAND_SHORT] for k in range(1, _CAND_SHORT)]
    pieces.append(sv0[_CAND_SHORT:] + sv1[0:1])
    return jnp.concatenate(pieces, axis=0)


def _route_kernel(tiles_a, xa_ref, xb_ref, wq_ref, sk_ref, e_ref, g_ref, q_scr):
    tt = xa_ref.shape[0]
    x1 = _two_part_tile(tiles_a, xa_ref, xb_ref)
    qf = jnp.dot(x1.astype(BF16), wq_ref[...], preferred_element_type=F32)
    for hc in range(2 * PEER_HEADS):
        q_scr[hc] = qf[:, hc * PEER_DHALF:(hc + 1) * PEER_DHALF]
    iota_n = lax.broadcasted_iota(I32, (N_KEYS, tt), 0)
    iota_k = lax.broadcasted_iota(I32, (PEER_TOPK, tt), 0)
    flat = _cand_flat_index(tt)

    def first_stage(h):
        out = []
        for c in range(2):
            s = _dot_nt(sk_ref[h, c], q_scr[2 * h + c])
            out.extend(_top16(s, iota_n, N_KEYS))
        return out[0], out[2], out[1], out[3]

    def second_stage(h, carry):
        sv0, sv1, si0, si1 = carry
        fv, fi = _top16(_cand_values(sv0, sv1), flat, PEER_TOPK * PEER_TOPK)
        i1 = _pick(si0, iota_k, lax.shift_right_logical(fi, 4))
        i2 = _pick(si1, iota_k, lax.bitwise_and(fi, PEER_TOPK - 1))
        ex = jnp.exp(fv - fv[0:1])
        row0 = pl.multiple_of(h * PEER_TOPK, PEER_TOPK)
        e_ref[0, pl.ds(row0, PEER_TOPK), :] = (i1 * N_KEYS + i2) * PACK_CHUNKS
        g_ref[0, pl.ds(row0, PEER_TOPK), :] = ex / jnp.sum(ex, axis=0, keepdims=True)

    def body(h, carry):
        second_stage(h - 1, carry)
        return first_stage(h)

    last = lax.fori_loop(1, PEER_HEADS, body, first_stage(0))
    second_stage(PEER_HEADS - 1, last)


def _route(x1a, x1b, wq_bf16, subkeys, tt):
    T = x1a.shape[0] + x1b.shape[0]
    tiles_a = x1a.shape[0] // tt
    full = lambda a: pl.BlockSpec(a.shape, lambda i: (0,) * a.ndim)
    out = pl.BlockSpec((1, PEER_PICKS, tt), lambda i: (i, 0, 0))
    return pl.pallas_call(
        functools.partial(_route_kernel, tiles_a),
        grid=(T // tt,),
        in_specs=[*_two_part_specs(tiles_a, tt, D_MODEL), full(wq_bf16), full(subkeys)],
        out_specs=[out, out],
        out_shape=[jax.ShapeDtypeStruct((T // tt, PEER_PICKS, tt), I32),
                   jax.ShapeDtypeStruct((T // tt, PEER_PICKS, tt), F32)],
        scratch_shapes=[pltpu.VMEM((2 * PEER_HEADS, tt, PEER_DHALF), F32)],
        compiler_params=pltpu.CompilerParams(
            dimension_semantics=("arbitrary",), vmem_limit_bytes=48 * MIB),
        name="peer_route",
    )(x1a, x1b, wq_bf16, subkeys)


def _pack_kernel(t_ref, o_ref):
    half = D_MODEL // 2
    lo = pltpu.bitcast(t_ref[:, :half].astype(BF16).astype(F32), jnp.uint32)
    hi = pltpu.bitcast(t_ref[:, half:].astype(BF16).astype(F32), jnp.uint32)
    word = (hi & jnp.uint32(0xFFFF0000)) | (lo >> 16)
    o_ref[...] = pltpu.bitcast(word, I32)


def _pack_table(tab, tm):
    n = tab.shape[0]
    packed = pl.pallas_call(
        _pack_kernel,
        grid=(n // tm,),
        in_specs=[pl.BlockSpec((tm, D_MODEL), lambda i: (i, 0))],
        out_specs=pl.BlockSpec((tm, D_MODEL // 2), lambda i: (i, 0)),
        out_shape=jax.ShapeDtypeStruct((n, D_MODEL // 2), I32),
        compiler_params=pltpu.CompilerParams(dimension_semantics=("arbitrary",)),
        name="pack_table",
    )(tab)
    return packed.reshape(n * PACK_CHUNKS, LANES)


def _unpack(word):
    lo = pltpu.bitcast(word << 16, F32)
    hi = pltpu.bitcast(word & jnp.int32(-65536), F32)
    return lo, hi


def _gather_group(idx_ref, tab_ref, tile_ref):
    for r in range(PEER_ROWS):
        row = pl.multiple_of(idx_ref[r], PACK_CHUNKS)
        tile_ref[pl.ds(r, PACK_CHUNKS, stride=PEER_PITCH), :] = tab_ref[pl.ds(row, PACK_CHUNKS), :]


def _tile_rows(tile_ref, c, j):
    return tile_ref[c * PEER_PITCH + j * PEER_GROUP:c * PEER_PITCH + (j + 1) * PEER_GROUP, :]


def _index_copy(e_hbm, group, buf, sem, slot):
    return pltpu.make_async_copy(e_hbm.at[group], buf, sem.at[slot])


def _for_each_group(e_hbm, idx_bufs, sem, groups_per_step, process):
    step = pl.program_id(0)
    total = pl.num_programs(0) * groups_per_step
    base = step * groups_per_step

    @pl.when(step == 0)
    def _():
        for slot in range(2):
            _index_copy(e_hbm, slot, idx_bufs[slot], sem, slot).start()

    def pair(p, carry):
        for slot in range(2):
            local = 2 * p + slot
            group = base + local
            _index_copy(e_hbm, group, idx_bufs[slot], sem, slot).wait()
            process(idx_bufs[slot], local)

            @pl.when(group + 2 < total)
            def _():
                _index_copy(e_hbm, group + 2, idx_bufs[slot], sem, slot).start()
        return carry

    lax.fori_loop(0, groups_per_step // 2, pair, 0)


def _peer_u_kernel(tiles_a, e_hbm, xa_ref, xb_ref, g_ref, tab_ref, w_ref, x_ref, tile_ref,
                   idx_a, idx_b, sem):
    lane = lax.broadcasted_iota(I32, (PEER_GROUP, PEER_PICKS), 1)
    x_ref[...] = _two_part_tile(tiles_a, xa_ref, xb_ref)

    def process(idx_ref, local):
        t0 = pl.multiple_of(local * PEER_GROUP, PEER_GROUP)
        _gather_group(idx_ref, tab_ref, tile_ref)
        xg = x_ref[pl.ds(t0, PEER_GROUP), :]
        a = jnp.zeros((PEER_GROUP, PEER_PICKS), F32)
        for j in range(PEER_PICKS):
            p = None
            for c in range(PACK_CHUNKS):
                lo, hi = _unpack(_tile_rows(tile_ref, c, j))
                term = (lo * xg[:, c * LANES:(c + 1) * LANES]
                        + hi * xg[:, D_MODEL // 2 + c * LANES:D_MODEL // 2 + (c + 1) * LANES])
                p = term if p is None else p + term
            a = jnp.where(lane == j, jnp.sum(p, axis=1, keepdims=True), a)
        w_ref[pl.ds(t0, PEER_GROUP), :] = g_ref[pl.ds(t0, PEER_GROUP), :] * jax.nn.gelu(a)

    _for_each_group(e_hbm, (idx_a, idx_b), sem, x_ref.shape[0] // PEER_GROUP, process)


def _peer_v_kernel(e_hbm, w_ref, tab_ref, f_ref, tile_ref, idx_a, idx_b, sem):
    def process(idx_ref, local):
        t0 = pl.multiple_of(local * PEER_GROUP, PEER_GROUP)
        _gather_group(idx_ref, tab_ref, tile_ref)
        wg = w_ref[pl.ds(t0, PEER_GROUP), :]
        acc = [jnp.zeros((PEER_GROUP, LANES), F32) for _ in range(2 * PACK_CHUNKS)]
        for j in range(PEER_PICKS):
            wb = jnp.broadcast_to(wg[:, j:j + 1], (PEER_GROUP, LANES))
            for c in range(PACK_CHUNKS):
                lo, hi = _unpack(_tile_rows(tile_ref, c, j))
                acc[c] = acc[c] + wb * lo
                acc[PACK_CHUNKS + c] = acc[PACK_CHUNKS + c] + wb * hi
        f_ref[pl.ds(t0, PEER_GROUP), :] = jnp.concatenate(acc, axis=1)

    _for_each_group(e_hbm, (idx_a, idx_b), sem, w_ref.shape[0] // PEER_GROUP, process)


def _peer_specs(tb):
    assert (tb // PEER_GROUP) % 2 == 0
    idx = pl.BlockSpec(memory_space=pl.ANY)
    picks = pl.BlockSpec((tb, PEER_PICKS), lambda i: (i, 0))
    feat = pl.BlockSpec((tb, D_MODEL), lambda i: (i, 0))
    table = pl.BlockSpec((N_EXPERTS * PACK_CHUNKS, LANES), lambda i: (0, 0),
                         pipeline_mode=pl.Buffered(1))
    scratch = [pltpu.VMEM((PACK_CHUNKS * PEER_PITCH, LANES), I32),
               pltpu.SMEM((PEER_ROWS,), I32), pltpu.SMEM((PEER_ROWS,), I32),
               pltpu.SemaphoreType.DMA((2,))]
    params = pltpu.CompilerParams(dimension_semantics=("arbitrary",),
                                  vmem_limit_bytes=48 * MIB)
    return idx, picks, feat, table, scratch, params


def _peer_u(e_grp, x1a, x1b, g, tab_u, tb):
    T = x1a.shape[0] + x1b.shape[0]
    tiles_a = x1a.shape[0] // tb
    idx, picks, feat, table, scratch, params = _peer_specs(tb)
    return pl.pallas_call(
        functools.partial(_peer_u_kernel, tiles_a), grid=(T // tb,),
        in_specs=[idx, *_two_part_specs(tiles_a, tb, D_MODEL), picks, table], out_specs=picks,
        out_shape=jax.ShapeDtypeStruct((T, PEER_PICKS), F32),
        scratch_shapes=[pltpu.VMEM((tb, D_MODEL), F32)] + scratch, compiler_params=params,
        name="peer_u",
    )(e_grp, x1a, x1b, g, tab_u)


def _peer_v(e_grp, w, tab_v, tb):
    T = w.shape[0]
    idx, picks, feat, table, scratch, params = _peer_specs(tb)
    return pl.pallas_call(
        _peer_v_kernel, grid=(T // tb,),
        in_specs=[idx, picks, table], out_specs=feat,
        out_shape=jax.ShapeDtypeStruct((T, D_MODEL), F32),
        scratch_shapes=scratch, compiler_params=params, name="peer_v",
    )(e_grp, w, tab_v)


def _final_kernel(x1_ref, f_ref, p_ref, wg_ref, wp_ref, g_ref, b_ref, y_ref):
    x1 = x1_ref[...]
    e = jax.nn.sigmoid(_dot(x1, wg_ref[...])) * _dot(p_ref[...], wp_ref[...])
    y_ref[...] = _layer_norm(ALPHA * x1 + f_ref[...] + e, g_ref[...], b_ref[...])


def _final(x1, f, f_row_offset, p, wg, wp, ln_g, ln_b, tm):
    T = p.shape[0]
    off = f_row_offset // tm
    full = lambda a: pl.BlockSpec(a.shape, lambda i: (0,) * a.ndim)
    shifted = pl.BlockSpec((tm, D_MODEL), lambda i: (i + off, 0))
    return pl.pallas_call(
        _final_kernel,
        grid=(T // tm,),
        in_specs=[pl.BlockSpec((tm, D_MODEL), lambda i: (i, 0)), shifted,
                  pl.BlockSpec((tm, PLE_DIM), lambda i: (i, 0)),
                  full(wg), full(wp), full(ln_g), full(ln_b)],
        out_specs=pl.BlockSpec((tm, D_MODEL), lambda i: (i, 0)),
        out_shape=jax.ShapeDtypeStruct((T, D_MODEL), F32),
        compiler_params=pltpu.CompilerParams(
            dimension_semantics=("arbitrary",), vmem_limit_bytes=48 * MIB),
        name="final",
    )(x1, f, p, wg, wp, ln_g, ln_b)


def _t5_bias(dist, table):
    n = jnp.maximum(dist, 0)
    max_exact = N_BUCKETS // 2
    nf = jnp.maximum(n, 1).astype(F32)
    large = max_exact + (jnp.log(nf / max_exact) / math.log(MAX_DISTANCE / max_exact)
                         * (N_BUCKETS - max_exact)).astype(I32)
    large = jnp.minimum(large, N_BUCKETS - 1)
    bucket = jnp.where(n < max_exact, n, large)
    onehot = (bucket[..., None] == jnp.arange(N_BUCKETS)).astype(F32)
    return jnp.einsum("qkb,bh->hqk", onehot, table.astype(F32), precision=lax.Precision.HIGHEST)


def kernel(x_prompt, x_sample, cache_k_win, cache_v_win, p_prompt, p_sample, rel_bias_table,
           w_in, attn_sinks, w_att_out, c_ln_g, c_ln_b, c_ws, c_bs, w_chunk_out, w_o, ln1_g,
           ln1_b, peer_wq, peer_subkeys, peer_u, peer_v, w_ple_gate, w_ple_proj, ln2_g, ln2_b):
    batch, seq, d = x_prompt.shape
    n_seq, s_new, _ = x_sample.shape
    w_buf = cache_k_win.shape[2]
    assert d == D_MODEL and w_in.shape[0] == DEPTH and seq % ATT_BLOCK == 0
    tp = batch * seq
    ts = n_seq * s_new
    t_all = tp + ts
    nb = seq // ATT_BLOCK
    seqs_per_step = 8
    assert n_seq % seqs_per_step == 0 and tp % (seqs_per_step * s_new) == 0
    assert tp % 256 == 0 and ts % 256 == 0 and tp % PEER_TOKENS_PER_STEP == 0

    row2 = lambda v: v.reshape(1, -1)
    w_in_b = w_in[0].astype(BF16)
    wa, wc, wo = w_att_out[0].astype(BF16), w_chunk_out[0].astype(BF16), w_o[0].astype(BF16)
    sinks = attn_sinks[0]
    cg, cb = row2(c_ln_g[0]), row2(c_ln_b[0])
    l1g, l1b = row2(ln1_g[0]), row2(ln1_b[0])
    l2g, l2b = row2(ln2_g[0]), row2(ln2_b[0])
    ws, bs = c_ws[0], c_bs[0]

    qi = jnp.arange(ATT_BLOCK)[:, None]
    kj = jnp.arange(2 * ATT_BLOCK)[None, :]
    bias_p = _t5_bias(qi - kj + ATT_BLOCK, rel_bias_table).astype(F32)
    qs = jnp.arange(s_new)[:, None]
    ks = jnp.arange(w_buf + s_new)[None, :]
    bias_s = _t5_bias(qs - (ks - w_buf), rel_bias_table).astype(F32)
    bias_s = bias_s.reshape(N_KV, REP * s_new, w_buf + s_new)
    bias_sc, bias_sn = bias_s[:, :, :w_buf], bias_s[:, :, w_buf:]

    bs_exp_p = jnp.repeat(bs.T, C_GDIM, axis=1)
    bs_exp_s = jnp.tile(jnp.repeat(bs[:, :s_new].T, C_GDIM, axis=1), (seqs_per_step, 1))
    tril = jnp.tril(jnp.ones((s_new, s_new), F32))
    eye = jnp.eye(seqs_per_step, dtype=F32)
    wbd = jnp.stack([jnp.kron(eye, ws[g, :s_new, :s_new] * tril) for g in range(C_GROUPS)])

    xp = x_prompt.reshape(tp, d)
    q, kv, cu, cvn, gate = _inproj(xp, w_in_b, cg, cb, 256)
    x1p = _prompt_blocks(nb, sinks, q, kv, cu, cvn, gate, xp, bias_p, ws, bs_exp_p,
                        wa, wc, wo, l1g, l1b)
    kv_tail = kv.reshape(batch, seq, 2 * KV_WIDTH)[:, seq - w_buf:]
    kp = kv_tail[..., :KV_WIDTH].reshape(1, batch, w_buf, N_KV, HEAD_DIM)
    vp = kv_tail[..., KV_WIDTH:].reshape(1, batch, w_buf, N_KV, HEAD_DIM)

    xs = x_sample.reshape(ts, d)
    q, kv, cu, cvn_s, gate = _inproj(xs, w_in_b, cg, cb, 256)
    x1s, k_new, v_new = _sample_blocks(
        seqs_per_step, s_new, sinks, q, kv, cu, cvn_s, gate, xs,
        cache_k_win[0].reshape(n_seq, w_buf, KV_WIDTH),
        cache_v_win[0].reshape(n_seq, w_buf, KV_WIDTH),
        bias_sc, bias_sn, wbd, bs_exp_s, wa, wc, wo, l1g, l1b)
    ks_out = k_new.reshape(1, n_seq, w_buf, N_KV, HEAD_DIM)
    vs_out = v_new.reshape(1, n_seq, w_buf, N_KV, HEAD_DIM)
    cs_out = cvn_s.reshape(1, n_seq, s_new, C_WIDTH)

    e_t, g_t = _route(x1p, x1s, peer_wq[0].astype(BF16), peer_subkeys[0], LANES)
    nt = t_all // LANES
    e_grp = (e_t.reshape(nt, PEER_PICKS, LANES // PEER_GROUP, PEER_GROUP)
             .transpose(0, 2, 1, 3).reshape(t_all // PEER_GROUP, PEER_ROWS))
    g_tok = g_t.transpose(0, 2, 1).reshape(t_all, PEER_PICKS)
    tab_u = _pack_table(peer_u[0], 1024)
    tab_v = _pack_table(peer_v[0], 1024)
    w_tok = _peer_u(e_grp, x1p, x1s, g_tok, tab_u, PEER_TOKENS_PER_STEP)
    f = _peer_v(e_grp, w_tok, tab_v, PEER_TOKENS_PER_STEP)

    wg, wp = w_ple_gate[0].astype(BF16), w_ple_proj[0].astype(BF16)
    yp = _final(x1p, f, 0, p_prompt[0].reshape(tp, PLE_DIM), wg, wp, l2g, l2b, 256)
    ys = _final(x1s, f, tp, p_sample[0].reshape(ts, PLE_DIM), wg, wp, l2g, l2b, 256)
    return (yp.reshape(batch, seq, d), ys.reshape(n_seq, s_new, d), kp, vp, ks_out, vs_out,
            cs_out)
```

```python
import functools
import math

import jax
import jax.numpy as jnp
from jax import lax
from jax.experimental import pallas as pl
from jax.experimental.pallas import tpu as pltpu

F32 = jnp.float32
BF16 = jnp.bfloat16
I32 = jnp.int32

D_MODEL = 1024
N_HEADS = 8
N_KV = 2
REP = N_HEADS // N_KV
HEAD_DIM = 64
Q_WIDTH = N_HEADS * HEAD_DIM
KV_WIDTH = N_KV * HEAD_DIM
WINDOW = 128
ATT_BLOCK = 128
N_BUCKETS = 32
MAX_DISTANCE = 128
C_GROUPS = 4
C_CHUNK = 128
C_WIDTH = 512
C_GDIM = C_WIDTH // C_GROUPS
PLE_DIM = 256
PEER_HEADS = 8
N_KEYS = 128
N_EXPERTS = N_KEYS * N_KEYS
PEER_TOPK = 16
PEER_DKEY = 256
PEER_DHALF = PEER_DKEY // 2
PEER_PICKS = PEER_HEADS * PEER_TOPK
DEPTH = 1
ALPHA = (2.0 * DEPTH) ** 0.25
LN_EPS = 1e-5
NEG = -1e30
IN_WIDTH = Q_WIDTH + 2 * KV_WIDTH + 2 * C_WIDTH + 2 * D_MODEL

LANES = 128
SUBLANES = 8
MIB = 1024 * 1024

PEER_GROUP = SUBLANES
PEER_ROWS = PEER_GROUP * PEER_PICKS
PACK_CHUNKS = D_MODEL // (2 * LANES)
PEER_TOKENS_PER_STEP = 256
ROUTE_TILE = 256


def _layer_norm(x, g, b):
    mu = jnp.mean(x, axis=-1, keepdims=True)
    xc = x - mu
    var = jnp.mean(xc * xc, axis=-1, keepdims=True)
    return xc * lax.rsqrt(var + LN_EPS) * g + b


def _dot(a, b):
    return jnp.dot(a.astype(BF16), b.astype(BF16), preferred_element_type=F32)


def _dot_nt(a, b):
    return lax.dot_general(a.astype(BF16), b.astype(BF16), (((1,), (1,)), ((), ())),
                           preferred_element_type=F32)


def _inproj_kernel(x_ref, w_ref, g_ref, b_ref, q_ref, kv_ref, cu_ref, cvn_ref, gate_ref):
    z = jnp.dot(x_ref[...].astype(BF16), w_ref[...], preferred_element_type=F32)
    o = 0
    q_ref[...] = z[:, o:o + Q_WIDTH]
    o += Q_WIDTH
    kv_ref[...] = z[:, o:o + 2 * KV_WIDTH]
    o += 2 * KV_WIDTH
    cu_ref[...] = z[:, o:o + C_WIDTH]
    o += C_WIDTH
    cvn_ref[...] = _layer_norm(z[:, o:o + C_WIDTH], g_ref[...], b_ref[...])
    o += C_WIDTH
    gate_ref[...] = z[:, o:o + 2 * D_MODEL]


def _inproj(x, w_bf16, c_g, c_b, tm):
    T = x.shape[0]
    widths = (Q_WIDTH, 2 * KV_WIDTH, C_WIDTH, C_WIDTH, 2 * D_MODEL)
    row = lambda n: pl.BlockSpec((tm, n), lambda i: (i, 0))
    full = lambda a: pl.BlockSpec(a.shape, lambda i: (0,) * a.ndim)
    return pl.pallas_call(
        _inproj_kernel,
        grid=(T // tm,),
        in_specs=[row(D_MODEL), full(w_bf16), full(c_g), full(c_b)],
        out_specs=[row(n) for n in widths],
        out_shape=[jax.ShapeDtypeStruct((T, n), F32) for n in widths],
        compiler_params=pltpu.CompilerParams(
            dimension_semantics=("arbitrary",), vmem_limit_bytes=48 * MIB),
        name="inproj",
    )(x, w_bf16, c_g, c_b)


def _merge_and_norm(x, ya, yc, gate, wa_ref, wc_ref, wo_ref, g_ref, b_ref):
    ga = gate[:, :D_MODEL]
    gc = gate[:, D_MODEL:]
    mix = jax.nn.sigmoid(ga) * _dot(ya, wa_ref[...]) + jax.nn.sigmoid(gc) * _dot(yc, wc_ref[...])
    h = ALPHA * x + _dot(mix, wo_ref[...])
    return _layer_norm(h, g_ref[...], b_ref[...])


def _softmax_with_sink(s, sink):
    m = jnp.maximum(jnp.max(s, axis=-1, keepdims=True), sink)
    e = jnp.exp(s - m)
    return e / (jnp.sum(e, axis=-1, keepdims=True) + jnp.exp(sink - m))


def _prompt_block_kernel(nb, sinks_ref, q_ref, kvp_ref, kvo_ref, cu_ref, cvn_ref, gate_ref,
                         x_ref, bias_ref, ws_ref, bs_ref, wa_ref, wc_ref, wo_ref, g_ref,
                         b_ref, x1_ref):
    n = lax.rem(pl.program_id(0), nb)
    blk = ATT_BLOCK
    qi = lax.broadcasted_iota(I32, (blk, 2 * blk), 0)
    kj = lax.broadcasted_iota(I32, (blk, 2 * blk), 1)
    dist = qi - kj + blk
    mask = (dist >= 0) & (dist < WINDOW) & ((kj >= blk) | (n > 0))

    kv = jnp.concatenate([kvp_ref[...], kvo_ref[...]], axis=0)
    q = q_ref[...]
    outs = []
    for h in range(N_HEADS):
        g = h // REP
        qh = q[:, h * HEAD_DIM:(h + 1) * HEAD_DIM]
        kg = kv[:, g * HEAD_DIM:(g + 1) * HEAD_DIM]
        vg = kv[:, KV_WIDTH + g * HEAD_DIM:KV_WIDTH + (g + 1) * HEAD_DIM]
        s = _dot_nt(qh, kg) * (HEAD_DIM ** -0.5) + bias_ref[h]
        s = jnp.where(mask, s, NEG)
        w = _softmax_with_sink(s, sinks_ref[h])
        outs.append(_dot(w, vg))
    ya = jnp.concatenate(outs, axis=1)

    ti = lax.broadcasted_iota(I32, (C_CHUNK, C_CHUNK), 0)
    si = lax.broadcasted_iota(I32, (C_CHUNK, C_CHUNK), 1)
    cvn = cvn_ref[...]
    parts = []
    for g in range(C_GROUPS):
        wm = jnp.where(si <= ti, ws_ref[g], 0.0)
        parts.append(_dot(wm, cvn[:, g * C_GDIM:(g + 1) * C_GDIM]))
    yc = cu_ref[...] * (jnp.concatenate(parts, axis=1) + bs_ref[...])

    x1_ref[...] = _merge_and_norm(x_ref[...], ya, yc, gate_ref[...], wa_ref, wc_ref, wo_ref,
                                  g_ref, b_ref)


def _prompt_blocks(nb, sinks, q, kv, cu, cvn, gate, x, bias, ws, bs_exp,
                   wa, wc, wo, ln_g, ln_b):
    T = q.shape[0]
    blk = ATT_BLOCK
    row = lambda n: pl.BlockSpec((blk, n), lambda i: (i, 0))
    full = lambda a: pl.BlockSpec(a.shape, lambda i: (0,) * a.ndim)
    prev = pl.BlockSpec((blk, 2 * KV_WIDTH), lambda i: (jnp.maximum(i - 1, 0), 0))
    return pl.pallas_call(
        functools.partial(_prompt_block_kernel, nb),
        grid=(T // blk,),
        in_specs=[pl.BlockSpec(memory_space=pltpu.SMEM),
                  row(Q_WIDTH), prev, row(2 * KV_WIDTH), row(C_WIDTH), row(C_WIDTH),
                  row(2 * D_MODEL), row(D_MODEL), full(bias), full(ws), full(bs_exp),
                  full(wa), full(wc), full(wo), full(ln_g), full(ln_b)],
        out_specs=row(D_MODEL),
        out_shape=jax.ShapeDtypeStruct((T, D_MODEL), F32),
        compiler_params=pltpu.CompilerParams(
            dimension_semantics=("arbitrary",), vmem_limit_bytes=48 * MIB),
        name="prompt_blocks",
    )(sinks, q, kv, kv, cu, cvn, gate, x, bias, ws, bs_exp, wa, wc, wo, ln_g, ln_b)


def _sample_block_kernel(seqs, s_new, sinks_ref, q_ref, kv_ref, cu_ref, cvn_ref, gate_ref,
                         x_ref, ck_ref, cv_ref, biasc_ref, biasn_ref, wbd_ref, bs_ref, wa_ref,
                         wc_ref, wo_ref, g_ref, b_ref, x1_ref, kout_ref, vout_ref):
    w_buf = ck_ref.shape[1]
    rows = REP * s_new
    qi_c = lax.rem(lax.broadcasted_iota(I32, (rows, w_buf), 0), s_new)
    kj_c = lax.broadcasted_iota(I32, (rows, w_buf), 1)
    mask_c = (qi_c + w_buf - kj_c) < WINDOW
    qi_n = lax.rem(lax.broadcasted_iota(I32, (rows, s_new), 0), s_new)
    kj_n = lax.broadcasted_iota(I32, (rows, s_new), 1)
    mask_n = kj_n <= qi_n
    sink_col = [jnp.concatenate([jnp.full((s_new, 1), sinks_ref[g * REP + r], F32)
                                 for r in range(REP)], axis=0) for g in range(N_KV)]

    q_all = q_ref[...]
    kv_all = kv_ref[...]
    ya_rows = []
    for b in range(seqs):
        qb = q_all[b * s_new:(b + 1) * s_new]
        kvb = kv_all[b * s_new:(b + 1) * s_new]
        ck = ck_ref[b]
        cv = cv_ref[b]
        kout_ref[b] = jnp.concatenate([ck[s_new:], kvb[:, :KV_WIDTH]], axis=0)
        vout_ref[b] = jnp.concatenate([cv[s_new:], kvb[:, KV_WIDTH:]], axis=0)
        heads = []
        for g in range(N_KV):
            qg = jnp.concatenate([qb[:, (g * REP + r) * HEAD_DIM:(g * REP + r + 1) * HEAD_DIM]
                                  for r in range(REP)], axis=0)
            lane = slice(g * HEAD_DIM, (g + 1) * HEAD_DIM)
            sc = _dot_nt(qg, ck[:, lane]) * (HEAD_DIM ** -0.5) + biasc_ref[g]
            sn = _dot_nt(qg, kvb[:, lane]) * (HEAD_DIM ** -0.5) + biasn_ref[g]
            sc = jnp.where(mask_c, sc, NEG)
            sn = jnp.where(mask_n, sn, NEG)
            sink = sink_col[g]
            m = jnp.maximum(jnp.maximum(jnp.max(sc, axis=-1, keepdims=True),
                                        jnp.max(sn, axis=-1, keepdims=True)), sink)
            ec = jnp.exp(sc - m)
            en = jnp.exp(sn - m)
            den = (jnp.sum(ec, axis=-1, keepdims=True) + jnp.sum(en, axis=-1, keepdims=True)
                   + jnp.exp(sink - m))
            vlane = slice(KV_WIDTH + g * HEAD_DIM, KV_WIDTH + (g + 1) * HEAD_DIM)
            og = _dot(ec / den, cv[:, lane]) + _dot(en / den, kvb[:, vlane])
            heads.extend(og[r * s_new:(r + 1) * s_new] for r in range(REP))
        ya_rows.append(jnp.concatenate(heads, axis=1))
    ya = jnp.concatenate(ya_rows, axis=0)

    cvn = cvn_ref[...]
    parts = [_dot(wbd_ref[g], cvn[:, g * C_GDIM:(g + 1) * C_GDIM]) for g in range(C_GROUPS)]
    yc = cu_ref[...] * (jnp.concatenate(parts, axis=1) + bs_ref[...])

    x1_ref[...] = _merge_and_norm(x_ref[...], ya, yc, gate_ref[...], wa_ref, wc_ref, wo_ref,
                                  g_ref, b_ref)


def _sample_blocks(seqs, s_new, sinks, q, kv, cu, cvn, gate, x, cache_k,
                   cache_v, bias_c, bias_n, wbd, bs_exp, wa, wc, wo, ln_g, ln_b):
    T = q.shape[0]
    n_seq, w_buf, kvw = cache_k.shape
    tm = seqs * s_new
    row = lambda n: pl.BlockSpec((tm, n), lambda i: (i, 0))
    full = lambda a: pl.BlockSpec(a.shape, lambda i: (0,) * a.ndim)
    cache = pl.BlockSpec((seqs, w_buf, kvw), lambda i: (i, 0, 0))
    return pl.pallas_call(
        functools.partial(_sample_block_kernel, seqs, s_new),
        grid=(T // tm,),
        in_specs=[pl.BlockSpec(memory_space=pltpu.SMEM),
                  row(Q_WIDTH), row(2 * KV_WIDTH), row(C_WIDTH), row(C_WIDTH),
                  row(2 * D_MODEL), row(D_MODEL), cache, cache, full(bias_c), full(bias_n),
                  full(wbd), full(bs_exp), full(wa), full(wc), full(wo), full(ln_g),
                  full(ln_b)],
        out_specs=[row(D_MODEL), cache, cache],
        out_shape=[jax.ShapeDtypeStruct((T, D_MODEL), F32),
                   jax.ShapeDtypeStruct(cache_k.shape, F32),
                   jax.ShapeDtypeStruct(cache_v.shape, F32)],
        compiler_params=pltpu.CompilerParams(
            dimension_semantics=("arbitrary",), vmem_limit_bytes=48 * MIB),
        name="sample_blocks",
    )(sinks, q, kv, cu, cvn, gate, x, cache_k, cache_v, bias_c, bias_n, wbd, bs_exp, wa, wc,
      wo, ln_g, ln_b)


def _two_part_specs(tiles_a, tile_rows, width):
    first = pl.BlockSpec((tile_rows, width), lambda i: (jnp.minimum(i, tiles_a - 1), 0))
    second = pl.BlockSpec((tile_rows, width), lambda i: (jnp.maximum(i - tiles_a, 0), 0))
    return first, second


def _two_part_tile(tiles_a, a_ref, b_ref):
    return jnp.where(pl.program_id(0) < tiles_a, a_ref[...], b_ref[...])


def _top16(s, iota, fill):
    vals, idxs = [], []
    for _ in range(PEER_TOPK):
        m = jnp.max(s, axis=0, keepdims=True)
        i = jnp.min(jnp.where(s == m, iota, fill), axis=0, keepdims=True)
        vals.append(m)
        idxs.append(i)
        s = jnp.where(iota == i, -jnp.inf, s)
    return jnp.concatenate(vals, axis=0), jnp.concatenate(idxs, axis=0)


def _pick(table, iota, idx):
    return jnp.concatenate(
        [jnp.sum(jnp.where(iota == idx[r:r + 1], table, 0), axis=0, keepdims=True)
         for r in range(PEER_TOPK)], axis=0)


_CAND_SHORT = SUBLANES


def _cand_flat_index(tt):
    i16 = lax.broadcasted_iota(I32, (PEER_TOPK, tt), 0)
    i8 = lax.broadcasted_iota(I32, (_CAND_SHORT, tt), 0)
    pieces = [i16] + [k * PEER_TOPK + i8 for k in range(1, _CAND_SHORT)]
    pieces.append((_CAND_SHORT + i8) * PEER_TOPK)
    return jnp.concatenate(pieces, axis=0)


def _cand_values(sv0, sv1):
    pieces = [sv0[0:1] + sv1]
    pieces += [sv0[k:k + 1] + sv1[0:_CAND_SHORT] for k in range(1, _CAND_SHORT)]
    pieces.append(sv0[_CAND_SHORT:] + sv1[0:1])
    return jnp.concatenate(pieces, axis=0)


def _route_kernel(tiles_a, xa_ref, xb_ref, wq_ref, sk_ref, e_ref, g_ref, q_scr):
    tt = xa_ref.shape[0]
    x1 = _two_part_tile(tiles_a, xa_ref, xb_ref)
    qf = jnp.dot(x1.astype(BF16), wq_ref[...], preferred_element_type=F32)
    for hc in range(2 * PEER_HEADS):
        q_scr[hc] = qf[:, hc * PEER_DHALF:(hc + 1) * PEER_DHALF]
    iota_n = lax.broadcasted_iota(I32, (N_KEYS, tt), 0)
    iota_k = lax.broadcasted_iota(I32, (PEER_TOPK, tt), 0)
    flat = _cand_flat_index(tt)

    def first_stage(h):
        out = []
        for c in range(2):
            s = _dot_nt(sk_ref[h, c], q_scr[2 * h + c])
            out.extend(_top16(s, iota_n, N_KEYS))
        return out[0], out[2], out[1], out[3]

    def second_stage(h, carry):
        sv0, sv1, si0, si1 = carry
        fv, fi = _top16(_cand_values(sv0, sv1), flat, PEER_TOPK * PEER_TOPK)
        i1 = _pick(si0, iota_k, lax.shift_right_logical(fi, 4))
        i2 = _pick(si1, iota_k, lax.bitwise_and(fi, PEER_TOPK - 1))
        ex = jnp.exp(fv - fv[0:1])
        row0 = pl.multiple_of(h * PEER_TOPK, PEER_TOPK)
        e_ref[0, pl.ds(row0, PEER_TOPK), :] = (i1 * N_KEYS + i2) * PACK_CHUNKS
        g_ref[0, pl.ds(row0, PEER_TOPK), :] = ex / jnp.sum(ex, axis=0, keepdims=True)

    def body(h, carry):
        second_stage(h - 1, carry)
        return first_stage(h)

    last = lax.fori_loop(1, PEER_HEADS, body, first_stage(0))
    second_stage(PEER_HEADS - 1, last)


def _route(x1a, x1b, wq_bf16, subkeys, tt):
    T = x1a.shape[0] + x1b.shape[0]
    tiles_a = x1a.shape[0] // tt
    full = lambda a: pl.BlockSpec(a.shape, lambda i: (0,) * a.ndim)
    out = pl.BlockSpec((1, PEER_PICKS, tt), lambda i: (i, 0, 0))
    return pl.pallas_call(
        functools.partial(_route_kernel, tiles_a),
        grid=(T // tt,),
        in_specs=[*_two_part_specs(tiles_a, tt, D_MODEL), full(wq_bf16), full(subkeys)],
        out_specs=[out, out],
        out_shape=[jax.ShapeDtypeStruct((T // tt, PEER_PICKS, tt), I32),
                   jax.ShapeDtypeStruct((T // tt, PEER_PICKS, tt), F32)],
        scratch_shapes=[pltpu.VMEM((2 * PEER_HEADS, tt, PEER_DHALF), F32)],
        compiler_params=pltpu.CompilerParams(
            dimension_semantics=("arbitrary",), vmem_limit_bytes=48 * MIB),
        name="peer_route",
    )(x1a, x1b, wq_bf16, subkeys)


def _pack_table(tab):
    n = tab.shape[0]
    pairs = tab.astype(BF16).reshape(n, 2, PACK_CHUNKS, LANES).transpose(0, 2, 3, 1)
    return lax.bitcast_convert_type(pairs, I32).reshape(n * PACK_CHUNKS, LANES)


def _unpack(word):
    return pltpu.bitcast(word, BF16).astype(F32)


def _interleave_store(ref, even_rows, odd_rows):
    n = even_rows.shape[0]
    ref[pl.ds(0, n, stride=2), :] = even_rows
    ref[pl.ds(1, n, stride=2), :] = odd_rows


def _gather_group(idx_ref, tab_ref, tile_ref):
    for r in range(PEER_ROWS):
        row = pl.multiple_of(idx_ref[r], PACK_CHUNKS)
        tile_ref[pl.ds(r * PACK_CHUNKS, PACK_CHUNKS), :] = tab_ref[pl.ds(row, PACK_CHUNKS), :]


def _tile_rows(tile_ref, c, j):
    start = j * PEER_GROUP * PACK_CHUNKS + c
    return tile_ref[pl.ds(start, PEER_GROUP, stride=PACK_CHUNKS), :]


def _index_copy(e_hbm, group, buf, sem, slot):
    return pltpu.make_async_copy(e_hbm.at[group], buf, sem.at[slot])


def _for_each_group(e_hbm, idx_bufs, sem, groups_per_step, process):
    step = pl.program_id(0)
    total = pl.num_programs(0) * groups_per_step
    base = step * groups_per_step

    @pl.when(step == 0)
    def _():
        for slot in range(2):
            _index_copy(e_hbm, slot, idx_bufs[slot], sem, slot).start()

    def pair(p, carry):
        for slot in range(2):
            local = 2 * p + slot
            group = base + local
            _index_copy(e_hbm, group, idx_bufs[slot], sem, slot).wait()
            process(idx_bufs[slot], local)

            @pl.when(group + 2 < total)
            def _():
                _index_copy(e_hbm, group + 2, idx_bufs[slot], sem, slot).start()
        return carry

    lax.fori_loop(0, groups_per_step // 2, pair, 0)


def _peer_u_kernel(tiles_a, e_hbm, xa_ref, xb_ref, g_ref, tab_ref, w_ref, x_ref, xi_ref, a2_ref,
                   tile_ref, idx_a, idx_b, sem):
    lane = lax.broadcasted_iota(I32, (2 * PEER_GROUP, PEER_PICKS), 1)
    x_ref[...] = _two_part_tile(tiles_a, xa_ref, xb_ref)
    half = D_MODEL // 2

    def process(idx_ref, local):
        t0 = pl.multiple_of(local * PEER_GROUP, PEER_GROUP)
        _gather_group(idx_ref, tab_ref, tile_ref)
        xg = x_ref[pl.ds(t0, PEER_GROUP), :]
        for c in range(PACK_CHUNKS):
            _interleave_store(xi_ref.at[c], xg[:, c * LANES:(c + 1) * LANES],
                              xg[:, half + c * LANES:half + (c + 1) * LANES])
        xi = [xi_ref[c] for c in range(PACK_CHUNKS)]
        a2 = jnp.zeros((2 * PEER_GROUP, PEER_PICKS), F32)
        for j in range(PEER_PICKS):
            p = None
            for c in range(PACK_CHUNKS):
                term = _unpack(_tile_rows(tile_ref, c, j)) * xi[c]
                p = term if p is None else p + term
            a2 = jnp.where(lane == j, jnp.sum(p, axis=1, keepdims=True), a2)
        a2_ref[...] = a2
        a = a2_ref[pl.ds(0, PEER_GROUP, stride=2), :] + a2_ref[pl.ds(1, PEER_GROUP, stride=2), :]
        w_ref[pl.ds(t0, PEER_GROUP), :] = g_ref[pl.ds(t0, PEER_GROUP), :] * jax.nn.gelu(a)

    _for_each_group(e_hbm, (idx_a, idx_b), sem, x_ref.shape[0] // PEER_GROUP, process)


def _peer_v_kernel(e_hbm, w_ref, tab_ref, f_ref, w2_ref, acc_ref, tile_ref, idx_a, idx_b, sem):
    half = D_MODEL // 2

    def process(idx_ref, local):
        t0 = pl.multiple_of(local * PEER_GROUP, PEER_GROUP)
        _gather_group(idx_ref, tab_ref, tile_ref)
        wg = w_ref[pl.ds(t0, PEER_GROUP), :]
        _interleave_store(w2_ref, wg, wg)
        w2 = w2_ref[...]
        acc = [jnp.zeros((2 * PEER_GROUP, LANES), F32) for _ in range(PACK_CHUNKS)]
        for j in range(PEER_PICKS):
            wb = jnp.broadcast_to(w2[:, j:j + 1], (2 * PEER_GROUP, LANES))
            for c in range(PACK_CHUNKS):
                acc[c] = acc[c] + wb * _unpack(_tile_rows(tile_ref, c, j))
        for c in range(PACK_CHUNKS):
            acc_ref[c] = acc[c]
            f_ref[pl.ds(t0, PEER_GROUP), c * LANES:(c + 1) * LANES] = (
                acc_ref[c, pl.ds(0, PEER_GROUP, stride=2), :])
            f_ref[pl.ds(t0, PEER_GROUP), half + c * LANES:half + (c + 1) * LANES] = (
                acc_ref[c, pl.ds(1, PEER_GROUP, stride=2), :])

    _for_each_group(e_hbm, (idx_a, idx_b), sem, w_ref.shape[0] // PEER_GROUP, process)


def _peer_specs(tb):
    assert (tb // PEER_GROUP) % 2 == 0
    idx = pl.BlockSpec(memory_space=pl.ANY)
    picks = pl.BlockSpec((tb, PEER_PICKS), lambda i: (i, 0))
    feat = pl.BlockSpec((tb, D_MODEL), lambda i: (i, 0))
    table = pl.BlockSpec((N_EXPERTS * PACK_CHUNKS, LANES), lambda i: (0, 0),
                         pipeline_mode=pl.Buffered(1))
    scratch = [pltpu.VMEM((PACK_CHUNKS * PEER_ROWS, LANES), I32),
               pltpu.SMEM((PEER_ROWS,), I32), pltpu.SMEM((PEER_ROWS,), I32),
               pltpu.SemaphoreType.DMA((2,))]
    params = pltpu.CompilerParams(dimension_semantics=("arbitrary",),
                                  vmem_limit_bytes=48 * MIB)
    return idx, picks, feat, table, scratch, params


def _peer_u(e_grp, x1a, x1b, g, tab_u, tb):
    T = x1a.shape[0] + x1b.shape[0]
    tiles_a = x1a.shape[0] // tb
    idx, picks, feat, table, scratch, params = _peer_specs(tb)
    return pl.pallas_call(
        functools.partial(_peer_u_kernel, tiles_a), grid=(T // tb,),
        in_specs=[idx, *_two_part_specs(tiles_a, tb, D_MODEL), picks, table], out_specs=picks,
        out_shape=jax.ShapeDtypeStruct((T, PEER_PICKS), F32),
        scratch_shapes=[pltpu.VMEM((tb, D_MODEL), F32),
                        pltpu.VMEM((PACK_CHUNKS, 2 * PEER_GROUP, LANES), F32),
                        pltpu.VMEM((2 * PEER_GROUP, PEER_PICKS), F32)] + scratch,
        compiler_params=params,
        name="peer_u",
    )(e_grp, x1a, x1b, g, tab_u)


def _peer_v(e_grp, w, tab_v, tb):
    T = w.shape[0]
    idx, picks, feat, table, scratch, params = _peer_specs(tb)
    return pl.pallas_call(
        _peer_v_kernel, grid=(T // tb,),
        in_specs=[idx, picks, table], out_specs=feat,
        out_shape=jax.ShapeDtypeStruct((T, D_MODEL), F32),
        scratch_shapes=[pltpu.VMEM((2 * PEER_GROUP, PEER_PICKS), F32),
                        pltpu.VMEM((PACK_CHUNKS, 2 * PEER_GROUP, LANES), F32)] + scratch,
        compiler_params=params, name="peer_v",
    )(e_grp, w, tab_v)


def _final_kernel(x1_ref, f_ref, p_ref, wg_ref, wp_ref, g_ref, b_ref, y_ref):
    x1 = x1_ref[...]
    e = jax.nn.sigmoid(_dot(x1, wg_ref[...])) * _dot(p_ref[...], wp_ref[...])
    y_ref[...] = _layer_norm(ALPHA * x1 + f_ref[...] + e, g_ref[...], b_ref[...])


def _final(x1, f, f_row_offset, p, wg, wp, ln_g, ln_b, tm):
    T = p.shape[0]
    off = f_row_offset // tm
    full = lambda a: pl.BlockSpec(a.shape, lambda i: (0,) * a.ndim)
    shifted = pl.BlockSpec((tm, D_MODEL), lambda i: (i + off, 0))
    return pl.pallas_call(
        _final_kernel,
        grid=(T // tm,),
        in_specs=[pl.BlockSpec((tm, D_MODEL), lambda i: (i, 0)), shifted,
                  pl.BlockSpec((tm, PLE_DIM), lambda i: (i, 0)),
                  full(wg), full(wp), full(ln_g), full(ln_b)],
        out_specs=pl.BlockSpec((tm, D_MODEL), lambda i: (i, 0)),
        out_shape=jax.ShapeDtypeStruct((T, D_MODEL), F32),
        compiler_params=pltpu.CompilerParams(
            dimension_semantics=("arbitrary",), vmem_limit_bytes=48 * MIB),
        name="final",
    )(x1, f, p, wg, wp, ln_g, ln_b)


def _t5_bias(dist, table):
    n = jnp.maximum(dist, 0)
    max_exact = N_BUCKETS // 2
    nf = jnp.maximum(n, 1).astype(F32)
    large = max_exact + jnp.floor(jnp.log(nf / max_exact) / math.log(MAX_DISTANCE / max_exact)
                                  * (N_BUCKETS - max_exact)).astype(I32)
    large = jnp.minimum(large, N_BUCKETS - 1)
    bucket = jnp.where(n < max_exact, n, large)
    hit = bucket[None, :, :, None] == jnp.arange(N_BUCKETS)
    return jnp.sum(jnp.where(hit, table.T[:, None, None, :], 0.0), axis=-1)


def kernel(x_prompt, x_sample, cache_k_win, cache_v_win, p_prompt, p_sample, rel_bias_table,
           w_in, attn_sinks, w_att_out, c_ln_g, c_ln_b, c_ws, c_bs, w_chunk_out, w_o, ln1_g,
           ln1_b, peer_wq, peer_subkeys, peer_u, peer_v, w_ple_gate, w_ple_proj, ln2_g, ln2_b):
    batch, seq, d = x_prompt.shape
    n_seq, s_new, _ = x_sample.shape
    w_buf = cache_k_win.shape[2]
    assert d == D_MODEL and w_in.shape[0] == DEPTH and seq % ATT_BLOCK == 0
    tp = batch * seq
    ts = n_seq * s_new
    t_all = tp + ts
    nb = seq // ATT_BLOCK
    seqs_per_step = 8
    assert n_seq % seqs_per_step == 0 and tp % (seqs_per_step * s_new) == 0
    assert tp % 256 == 0 and ts % 256 == 0 and tp % PEER_TOKENS_PER_STEP == 0

    row2 = lambda v: v.reshape(1, -1)
    w_in_b = w_in[0].astype(BF16)
    wa, wc, wo = w_att_out[0].astype(BF16), w_chunk_out[0].astype(BF16), w_o[0].astype(BF16)
    sinks = attn_sinks[0]
    cg, cb = row2(c_ln_g[0]), row2(c_ln_b[0])
    l1g, l1b = row2(ln1_g[0]), row2(ln1_b[0])
    l2g, l2b = row2(ln2_g[0]), row2(ln2_b[0])
    ws, bs = c_ws[0], c_bs[0]

    qi = jnp.arange(ATT_BLOCK)[:, None]
    kj = jnp.arange(2 * ATT_BLOCK)[None, :]
    bias_p = _t5_bias(qi - kj + ATT_BLOCK, rel_bias_table).astype(F32)
    qs = jnp.arange(s_new)[:, None]
    ks = jnp.arange(w_buf + s_new)[None, :]
    bias_s = _t5_bias(qs - (ks - w_buf), rel_bias_table).astype(F32)
    bias_s = bias_s.reshape(N_KV, REP * s_new, w_buf + s_new)
    bias_sc, bias_sn = bias_s[:, :, :w_buf], bias_s[:, :, w_buf:]

    bs_exp_p = jnp.repeat(bs.T, C_GDIM, axis=1)
    bs_exp_s = jnp.tile(jnp.repeat(bs[:, :s_new].T, C_GDIM, axis=1), (seqs_per_step, 1))
    tril = jnp.tril(jnp.ones((s_new, s_new), F32))
    eye = jnp.eye(seqs_per_step, dtype=F32)
    wbd = jnp.stack([jnp.kron(eye, ws[g, :s_new, :s_new] * tril) for g in range(C_GROUPS)])

    xp = x_prompt.reshape(tp, d)
    q, kv, cu, cvn, gate = _inproj(xp, w_in_b, cg, cb, 256)
    x1p = _prompt_blocks(nb, sinks, q, kv, cu, cvn, gate, xp, bias_p, ws, bs_exp_p,
                        wa, wc, wo, l1g, l1b)
    kv_tail = kv.reshape(batch, seq, 2 * KV_WIDTH)[:, seq - w_buf:]
    kp = kv_tail[..., :KV_WIDTH].reshape(1, batch, w_buf, N_KV, HEAD_DIM)
    vp = kv_tail[..., KV_WIDTH:].reshape(1, batch, w_buf, N_KV, HEAD_DIM)

    xs = x_sample.reshape(ts, d)
    q, kv, cu, cvn_s, gate = _inproj(xs, w_in_b, cg, cb, 256)
    x1s, k_new, v_new = _sample_blocks(
        seqs_per_step, s_new, sinks, q, kv, cu, cvn_s, gate, xs,
        cache_k_win[0].reshape(n_seq, w_buf, KV_WIDTH),
        cache_v_win[0].reshape(n_seq, w_buf, KV_WIDTH),
        bias_sc, bias_sn, wbd, bs_exp_s, wa, wc, wo, l1g, l1b)
    ks_out = k_new.reshape(1, n_seq, w_buf, N_KV, HEAD_DIM)
    vs_out = v_new.reshape(1, n_seq, w_buf, N_KV, HEAD_DIM)
    cs_out = cvn_s.reshape(1, n_seq, s_new, C_WIDTH)

    e_t, g_t = _route(x1p, x1s, peer_wq[0].astype(BF16), peer_subkeys[0], ROUTE_TILE)
    nt = t_all // ROUTE_TILE
    e_grp = (e_t.reshape(nt, PEER_PICKS, ROUTE_TILE // PEER_GROUP, PEER_GROUP)
             .transpose(0, 2, 1, 3).reshape(t_all // PEER_GROUP, PEER_ROWS))
    g_tok = g_t.transpose(0, 2, 1).reshape(t_all, PEER_PICKS)
    tab_u = _pack_table(peer_u[0])
    tab_v = _pack_table(peer_v[0])
    w_tok = _peer_u(e_grp, x1p, x1s, g_tok, tab_u, PEER_TOKENS_PER_STEP)
    f = _peer_v(e_grp, w_tok, tab_v, PEER_TOKENS_PER_STEP)

    wg, wp = w_ple_gate[0].astype(BF16), w_ple_proj[0].astype(BF16)
    yp = _final(x1p, f, 0, p_prompt[0].reshape(tp, PLE_DIM), wg, wp, l2g, l2b, 256)
    ys = _final(x1s, f, tp, p_sample[0].reshape(ts, PLE_DIM), wg, wp, l2g, l2b, 256)
    return (yp.reshape(batch, seq, d), ys.reshape(n_seq, s_new, d), kp, vp, ks_out, vs_out,
            cs_out)
```

```python
import functools
import math

import jax
import jax.numpy as jnp
from jax import lax
from jax.experimental import pallas as pl
from jax.experimental.pallas import tpu as pltpu

F32 = jnp.float32
BF16 = jnp.bfloat16
I32 = jnp.int32

D_MODEL = 1024
N_HEADS = 8
N_KV = 2
REP = N_HEADS // N_KV
HEAD_DIM = 64
Q_WIDTH = N_HEADS * HEAD_DIM
KV_WIDTH = N_KV * HEAD_DIM
WINDOW = 128
ATT_BLOCK = 128
N_BUCKETS = 32
MAX_DISTANCE = 128
C_GROUPS = 4
C_CHUNK = 128
C_WIDTH = 512
C_GDIM = C_WIDTH // C_GROUPS
PLE_DIM = 256
PEER_HEADS = 8
N_KEYS = 128
N_EXPERTS = N_KEYS * N_KEYS
PEER_TOPK = 16
PEER_DKEY = 256
PEER_DHALF = PEER_DKEY // 2
PEER_PICKS = PEER_HEADS * PEER_TOPK
DEPTH = 1
ALPHA = (2.0 * DEPTH) ** 0.25
LN_EPS = 1e-5
NEG = -1e30
IN_WIDTH = Q_WIDTH + 2 * KV_WIDTH + 2 * C_WIDTH + 2 * D_MODEL

LANES = 128
SUBLANES = 8
MIB = 1024 * 1024

PEER_GROUP = SUBLANES
PEER_ROWS = PEER_GROUP * PEER_PICKS
PACK_CHUNKS = D_MODEL // (2 * LANES)
PEER_TOKENS_PER_STEP = 256
ROUTE_TILE = 256


def _layer_norm(x, g, b):
    mu = jnp.mean(x, axis=-1, keepdims=True)
    xc = x - mu
    var = jnp.mean(xc * xc, axis=-1, keepdims=True)
    return xc * lax.rsqrt(var + LN_EPS) * g + b


def _dot(a, b):
    return jnp.dot(a.astype(BF16), b.astype(BF16), preferred_element_type=F32)


def _dot_nt(a, b):
    return lax.dot_general(a.astype(BF16), b.astype(BF16), (((1,), (1,)), ((), ())),
                           preferred_element_type=F32)


def _inproj_kernel(x_ref, w_ref, g_ref, b_ref, q_ref, kv_ref, cu_ref, cvn_ref, gate_ref):
    z = jnp.dot(x_ref[...].astype(BF16), w_ref[...], preferred_element_type=F32)
    o = 0
    q_ref[...] = z[:, o:o + Q_WIDTH]
    o += Q_WIDTH
    kv_ref[...] = z[:, o:o + 2 * KV_WIDTH]
    o += 2 * KV_WIDTH
    cu_ref[...] = z[:, o:o + C_WIDTH]
    o += C_WIDTH
    cvn_ref[...] = _layer_norm(z[:, o:o + C_WIDTH], g_ref[...], b_ref[...])
    o += C_WIDTH
    gate_ref[...] = z[:, o:o + 2 * D_MODEL]


def _inproj(x, w_bf16, c_g, c_b, tm):
    T = x.shape[0]
    widths = (Q_WIDTH, 2 * KV_WIDTH, C_WIDTH, C_WIDTH, 2 * D_MODEL)
    row = lambda n: pl.BlockSpec((tm, n), lambda i: (i, 0))
    full = lambda a: pl.BlockSpec(a.shape, lambda i: (0,) * a.ndim)
    return pl.pallas_call(
        _inproj_kernel,
        grid=(T // tm,),
        in_specs=[row(D_MODEL), full(w_bf16), full(c_g), full(c_b)],
        out_specs=[row(n) for n in widths],
        out_shape=[jax.ShapeDtypeStruct((T, n), F32) for n in widths],
        compiler_params=pltpu.CompilerParams(
            dimension_semantics=("arbitrary",), vmem_limit_bytes=48 * MIB),
        name="inproj",
    )(x, w_bf16, c_g, c_b)


def _merge_and_norm(x, ya, yc, gate, wa_ref, wc_ref, wo_ref, g_ref, b_ref):
    ga = gate[:, :D_MODEL]
    gc = gate[:, D_MODEL:]
    mix = jax.nn.sigmoid(ga) * _dot(ya, wa_ref[...]) + jax.nn.sigmoid(gc) * _dot(yc, wc_ref[...])
    h = ALPHA * x + _dot(mix, wo_ref[...])
    return _layer_norm(h, g_ref[...], b_ref[...])


def _softmax_with_sink(s, sink):
    m = jnp.maximum(jnp.max(s, axis=-1, keepdims=True), sink)
    e = jnp.exp(s - m)
    return e / (jnp.sum(e, axis=-1, keepdims=True) + jnp.exp(sink - m))


PROMPT_BLOCKS_PER_STEP = 2


def _prompt_block_kernel(nb, sinks_ref, q_ref, kvp_ref, kvo_ref, cu_ref, cvn_ref, gate_ref,
                         x_ref, bias_ref, ws_ref, bs_ref, wa_ref, wc_ref, wo_ref, g_ref,
                         b_ref, x1_ref):
    blk = ATT_BLOCK
    qi = lax.broadcasted_iota(I32, (blk, 2 * blk), 0)
    kj = lax.broadcasted_iota(I32, (blk, 2 * blk), 1)
    dist = qi - kj + blk
    in_window = (dist >= 0) & (dist < WINDOW)
    ti = lax.broadcasted_iota(I32, (C_CHUNK, C_CHUNK), 0)
    si = lax.broadcasted_iota(I32, (C_CHUNK, C_CHUNK), 1)
    wms = [jnp.where(si <= ti, ws_ref[g], 0.0) for g in range(C_GROUPS)]

    ya_blocks, yc_blocks = [], []
    for sb in range(PROMPT_BLOCKS_PER_STEP):
        rows = slice(sb * blk, (sb + 1) * blk)
        n = lax.rem(pl.program_id(0) * PROMPT_BLOCKS_PER_STEP + sb, nb)
        mask = in_window & ((kj >= blk) | (n > 0))
        kv_prev = kvp_ref[...] if sb == 0 else kvo_ref[(sb - 1) * blk:sb * blk, :]
        kv = jnp.concatenate([kv_prev, kvo_ref[rows, :]], axis=0)
        q = q_ref[rows, :]
        outs = []
        for h in range(N_HEADS):
            g = h // REP
            qh = q[:, h * HEAD_DIM:(h + 1) * HEAD_DIM]
            kg = kv[:, g * HEAD_DIM:(g + 1) * HEAD_DIM]
            vg = kv[:, KV_WIDTH + g * HEAD_DIM:KV_WIDTH + (g + 1) * HEAD_DIM]
            s = _dot_nt(qh, kg) * (HEAD_DIM ** -0.5) + bias_ref[h]
            s = jnp.where(mask, s, NEG)
            w = _softmax_with_sink(s, sinks_ref[h])
            outs.append(_dot(w, vg))
        ya_blocks.append(jnp.concatenate(outs, axis=1))

        cvn = cvn_ref[rows, :]
        parts = [_dot(wms[g], cvn[:, g * C_GDIM:(g + 1) * C_GDIM]) for g in range(C_GROUPS)]
        yc_blocks.append(cu_ref[rows, :] * (jnp.concatenate(parts, axis=1) + bs_ref[...]))

    ya = jnp.concatenate(ya_blocks, axis=0)
    yc = jnp.concatenate(yc_blocks, axis=0)
    x1_ref[...] = _merge_and_norm(x_ref[...], ya, yc, gate_ref[...], wa_ref, wc_ref, wo_ref,
                                  g_ref, b_ref)


def _prompt_blocks(nb, sinks, q, kv, cu, cvn, gate, x, bias, ws, bs_exp,
                   wa, wc, wo, ln_g, ln_b):
    T = q.shape[0]
    blk = ATT_BLOCK
    tm = PROMPT_BLOCKS_PER_STEP * blk
    assert nb % PROMPT_BLOCKS_PER_STEP == 0
    row = lambda n: pl.BlockSpec((tm, n), lambda i: (i, 0))
    full = lambda a: pl.BlockSpec(a.shape, lambda i: (0,) * a.ndim)
    prev = pl.BlockSpec((blk, 2 * KV_WIDTH),
                        lambda i: (jnp.maximum(i * PROMPT_BLOCKS_PER_STEP - 1, 0), 0))
    return pl.pallas_call(
        functools.partial(_prompt_block_kernel, nb),
        grid=(T // tm,),
        in_specs=[pl.BlockSpec(memory_space=pltpu.SMEM),
                  row(Q_WIDTH), prev, row(2 * KV_WIDTH), row(C_WIDTH), row(C_WIDTH),
                  row(2 * D_MODEL), row(D_MODEL), full(bias), full(ws), full(bs_exp),
                  full(wa), full(wc), full(wo), full(ln_g), full(ln_b)],
        out_specs=row(D_MODEL),
        out_shape=jax.ShapeDtypeStruct((T, D_MODEL), F32),
        compiler_params=pltpu.CompilerParams(
            dimension_semantics=("arbitrary",), vmem_limit_bytes=48 * MIB),
        name="prompt_blocks",
    )(sinks, q, kv, kv, cu, cvn, gate, x, bias, ws, bs_exp, wa, wc, wo, ln_g, ln_b)


def _sample_block_kernel(seqs, s_new, sinks_ref, q_ref, kv_ref, cu_ref, cvn_ref, gate_ref,
                         x_ref, ck_ref, cv_ref, biasc_ref, biasn_ref, wbd_ref, bs_ref, wa_ref,
                         wc_ref, wo_ref, g_ref, b_ref, x1_ref, kout_ref, vout_ref):
    w_buf = ck_ref.shape[1]
    rows = REP * s_new
    qi_c = lax.rem(lax.broadcasted_iota(I32, (rows, w_buf), 0), s_new)
    kj_c = lax.broadcasted_iota(I32, (rows, w_buf), 1)
    mask_c = (qi_c + w_buf - kj_c) < WINDOW
    qi_n = lax.rem(lax.broadcasted_iota(I32, (rows, s_new), 0), s_new)
    kj_n = lax.broadcasted_iota(I32, (rows, s_new), 1)
    mask_n = kj_n <= qi_n
    sink_col = [jnp.concatenate([jnp.full((s_new, 1), sinks_ref[g * REP + r], F32)
                                 for r in range(REP)], axis=0) for g in range(N_KV)]

    q_all = q_ref[...]
    kv_all = kv_ref[...]
    ya_rows = []
    for b in range(seqs):
        qb = q_all[b * s_new:(b + 1) * s_new]
        kvb = kv_all[b * s_new:(b + 1) * s_new]
        ck = ck_ref[b]
        cv = cv_ref[b]
        kout_ref[b] = jnp.concatenate([ck[s_new:], kvb[:, :KV_WIDTH]], axis=0)
        vout_ref[b] = jnp.concatenate([cv[s_new:], kvb[:, KV_WIDTH:]], axis=0)
        heads = []
        for g in range(N_KV):
            qg = jnp.concatenate([qb[:, (g * REP + r) * HEAD_DIM:(g * REP + r + 1) * HEAD_DIM]
                                  for r in range(REP)], axis=0)
            lane = slice(g * HEAD_DIM, (g + 1) * HEAD_DIM)
            sc = _dot_nt(qg, ck[:, lane]) * (HEAD_DIM ** -0.5) + biasc_ref[g]
            sn = _dot_nt(qg, kvb[:, lane]) * (HEAD_DIM ** -0.5) + biasn_ref[g]
            sc = jnp.where(mask_c, sc, NEG)
            sn = jnp.where(mask_n, sn, NEG)
            sink = sink_col[g]
            m = jnp.maximum(jnp.maximum(jnp.max(sc, axis=-1, keepdims=True),
                                        jnp.max(sn, axis=-1, keepdims=True)), sink)
            ec = jnp.exp(sc - m)
            en = jnp.exp(sn - m)
            den = (jnp.sum(ec, axis=-1, keepdims=True) + jnp.sum(en, axis=-1, keepdims=True)
                   + jnp.exp(sink - m))
            vlane = slice(KV_WIDTH + g * HEAD_DIM, KV_WIDTH + (g + 1) * HEAD_DIM)
            og = _dot(ec / den, cv[:, lane]) + _dot(en / den, kvb[:, vlane])
            heads.extend(og[r * s_new:(r + 1) * s_new] for r in range(REP))
        ya_rows.append(jnp.concatenate(heads, axis=1))
    ya = jnp.concatenate(ya_rows, axis=0)

    cvn = cvn_ref[...]
    parts = [_dot(wbd_ref[g], cvn[:, g * C_GDIM:(g + 1) * C_GDIM]) for g in range(C_GROUPS)]
    yc = cu_ref[...] * (jnp.concatenate(parts, axis=1) + bs_ref[...])

    x1_ref[...] = _merge_and_norm(x_ref[...], ya, yc, gate_ref[...], wa_ref, wc_ref, wo_ref,
                                  g_ref, b_ref)


def _sample_blocks(seqs, s_new, sinks, q, kv, cu, cvn, gate, x, cache_k,
                   cache_v, bias_c, bias_n, wbd, bs_exp, wa, wc, wo, ln_g, ln_b):
    T = q.shape[0]
    n_seq, w_buf, kvw = cache_k.shape
    tm = seqs * s_new
    row = lambda n: pl.BlockSpec((tm, n), lambda i: (i, 0))
    full = lambda a: pl.BlockSpec(a.shape, lambda i: (0,) * a.ndim)
    cache = pl.BlockSpec((seqs, w_buf, kvw), lambda i: (i, 0, 0))
    return pl.pallas_call(
        functools.partial(_sample_block_kernel, seqs, s_new),
        grid=(T // tm,),
        in_specs=[pl.BlockSpec(memory_space=pltpu.SMEM),
                  row(Q_WIDTH), row(2 * KV_WIDTH), row(C_WIDTH), row(C_WIDTH),
                  row(2 * D_MODEL), row(D_MODEL), cache, cache, full(bias_c), full(bias_n),
                  full(wbd), full(bs_exp), full(wa), full(wc), full(wo), full(ln_g),
                  full(ln_b)],
        out_specs=[row(D_MODEL), cache, cache],
        out_shape=[jax.ShapeDtypeStruct((T, D_MODEL), F32),
                   jax.ShapeDtypeStruct(cache_k.shape, F32),
                   jax.ShapeDtypeStruct(cache_v.shape, F32)],
        compiler_params=pltpu.CompilerParams(
            dimension_semantics=("arbitrary",), vmem_limit_bytes=48 * MIB),
        name="sample_blocks",
    )(sinks, q, kv, cu, cvn, gate, x, cache_k, cache_v, bias_c, bias_n, wbd, bs_exp, wa, wc,
      wo, ln_g, ln_b)


def _two_part_specs(tiles_a, tile_rows, width):
    first = pl.BlockSpec((tile_rows, width), lambda i: (jnp.minimum(i, tiles_a - 1), 0))
    second = pl.BlockSpec((tile_rows, width), lambda i: (jnp.maximum(i - tiles_a, 0), 0))
    return first, second


def _two_part_tile(tiles_a, a_ref, b_ref):
    return jnp.where(pl.program_id(0) < tiles_a, a_ref[...], b_ref[...])


def _top16(s, iota, fill):
    vals, idxs = [], []
    for _ in range(PEER_TOPK):
        m = jnp.max(s, axis=0, keepdims=True)
        i = jnp.min(jnp.where(s == m, iota, fill), axis=0, keepdims=True)
        vals.append(m)
        idxs.append(i)
        s = jnp.where(iota == i, -jnp.inf, s)
    return jnp.concatenate(vals, axis=0), jnp.concatenate(idxs, axis=0)


def _pick(table, iota, idx):
    return jnp.concatenate(
        [jnp.sum(jnp.where(iota == idx[r:r + 1], table, 0), axis=0, keepdims=True)
         for r in range(PEER_TOPK)], axis=0)


_CAND_SHORT = SUBLANES


def _cand_flat_index(tt):
    i16 = lax.broadcasted_iota(I32, (PEER_TOPK, tt), 0)
    i8 = lax.broadcasted_iota(I32, (_CAND_SHORT, tt), 0)
    pieces = [i16] + [k * PEER_TOPK + i8 for k in range(1, _CAND_SHORT)]
    pieces.append((_CAND_SHORT + i8) * PEER_TOPK)
    return jnp.concatenate(pieces, axis=0)


def _cand_values(sv0, sv1):
    pieces = [sv0[0:1] + sv1]
    pieces += [sv0[k:k + 1] + sv1[0:_CAND_SHORT] for k in range(1, _CAND_SHORT)]
    pieces.append(sv0[_CAND_SHORT:] + sv1[0:1])
    return jnp.concatenate(pieces, axis=0)


def _route_kernel(tiles_a, xa_ref, xb_ref, wq_ref, sk_ref, e_ref, g_ref, q_scr):
    tt = xa_ref.shape[0]
    x1 = _two_part_tile(tiles_a, xa_ref, xb_ref)
    qf = jnp.dot(x1.astype(BF16), wq_ref[...], preferred_element_type=F32)
    for hc in range(2 * PEER_HEADS):
        q_scr[hc] = qf[:, hc * PEER_DHALF:(hc + 1) * PEER_DHALF]
    iota_n = lax.broadcasted_iota(I32, (N_KEYS, tt), 0)
    iota_k = lax.broadcasted_iota(I32, (PEER_TOPK, tt), 0)
    flat = _cand_flat_index(tt)

    def first_stage(h):
        out = []
        for c in range(2):
            s = _dot_nt(sk_ref[h, c], q_scr[2 * h + c])
            out.extend(_top16(s, iota_n, N_KEYS))
        return out[0], out[2], out[1], out[3]

    def second_stage(h, carry):
        sv0, sv1, si0, si1 = carry
        fv, fi = _top16(_cand_values(sv0, sv1), flat, PEER_TOPK * PEER_TOPK)
        i1 = _pick(si0, iota_k, lax.shift_right_logical(fi, 4))
        i2 = _pick(si1, iota_k, lax.bitwise_and(fi, PEER_TOPK - 1))
        ex = jnp.exp(fv - fv[0:1])
        row0 = pl.multiple_of(h * PEER_TOPK, PEER_TOPK)
        e_ref[0, pl.ds(row0, PEER_TOPK), :] = (i1 * N_KEYS + i2) * PACK_CHUNKS
        g_ref[0, pl.ds(row0, PEER_TOPK), :] = ex / jnp.sum(ex, axis=0, keepdims=True)

    def body(h, carry):
        second_stage(h - 1, carry)
        return first_stage(h)

    last = lax.fori_loop(1, PEER_HEADS, body, first_stage(0))
    second_stage(PEER_HEADS - 1, last)


def _route(x1a, x1b, wq_bf16, subkeys, tt):
    T = x1a.shape[0] + x1b.shape[0]
    tiles_a = x1a.shape[0] // tt
    full = lambda a: pl.BlockSpec(a.shape, lambda i: (0,) * a.ndim)
    out = pl.BlockSpec((1, PEER_PICKS, tt), lambda i: (i, 0, 0))
    return pl.pallas_call(
        functools.partial(_route_kernel, tiles_a),
        grid=(T // tt,),
        in_specs=[*_two_part_specs(tiles_a, tt, D_MODEL), full(wq_bf16), full(subkeys)],
        out_specs=[out, out],
        out_shape=[jax.ShapeDtypeStruct((T // tt, PEER_PICKS, tt), I32),
                   jax.ShapeDtypeStruct((T // tt, PEER_PICKS, tt), F32)],
        scratch_shapes=[pltpu.VMEM((2 * PEER_HEADS, tt, PEER_DHALF), F32)],
        compiler_params=pltpu.CompilerParams(
            dimension_semantics=("arbitrary",), vmem_limit_bytes=48 * MIB),
        name="peer_route",
    )(x1a, x1b, wq_bf16, subkeys)


def _pack_kernel(t_ref, o_ref, rows_ref):
    tm = t_ref.shape[0]
    half = D_MODEL // 2
    for c in range(PACK_CHUNKS):
        rows_ref[pl.ds(2 * c, tm, stride=2 * PACK_CHUNKS), :] = t_ref[:, c * LANES:(c + 1) * LANES]
        rows_ref[pl.ds(2 * c + 1, tm, stride=2 * PACK_CHUNKS), :] = (
            t_ref[:, half + c * LANES:half + (c + 1) * LANES])
    o_ref[...] = pltpu.bitcast(rows_ref[...].astype(BF16), I32)


def _pack_table(tab, tm):
    n = tab.shape[0]
    return pl.pallas_call(
        _pack_kernel,
        grid=(n // tm,),
        in_specs=[pl.BlockSpec((tm, D_MODEL), lambda i: (i, 0))],
        out_specs=pl.BlockSpec((tm * PACK_CHUNKS, LANES), lambda i: (i, 0)),
        out_shape=jax.ShapeDtypeStruct((n * PACK_CHUNKS, LANES), I32),
        scratch_shapes=[pltpu.VMEM((2 * tm * PACK_CHUNKS, LANES), F32)],
        compiler_params=pltpu.CompilerParams(dimension_semantics=("arbitrary",)),
        name="pack_table",
    )(tab)


def _unpack(word):
    return pltpu.bitcast(word, BF16).astype(F32)


def _interleave_store(ref, even_rows, odd_rows):
    n = even_rows.shape[0]
    ref[pl.ds(0, n, stride=2), :] = even_rows
    ref[pl.ds(1, n, stride=2), :] = odd_rows


def _gather_group(idx_ref, tab_ref, tile_ref):
    for r in range(PEER_ROWS):
        row = pl.multiple_of(idx_ref[r], PACK_CHUNKS)
        tile_ref[pl.ds(r * PACK_CHUNKS, PACK_CHUNKS), :] = tab_ref[pl.ds(row, PACK_CHUNKS), :]


def _tile_rows(tile_ref, c, j):
    start = j * PEER_GROUP * PACK_CHUNKS + c
    return tile_ref[pl.ds(start, PEER_GROUP, stride=PACK_CHUNKS), :]


def _index_copy(e_hbm, group, buf, sem, slot):
    return pltpu.make_async_copy(e_hbm.at[group], buf, sem.at[slot])


def _for_each_group(e_hbm, idx_bufs, sem, groups_per_step, process):
    step = pl.program_id(0)
    total = pl.num_programs(0) * groups_per_step
    base = step * groups_per_step

    @pl.when(step == 0)
    def _():
        for slot in range(2):
            _index_copy(e_hbm, slot, idx_bufs[slot], sem, slot).start()

    def pair(p, carry):
        for slot in range(2):
            local = 2 * p + slot
            group = base + local
            _index_copy(e_hbm, group, idx_bufs[slot], sem, slot).wait()
            process(idx_bufs[slot], local)

            @pl.when(group + 2 < total)
            def _():
                _index_copy(e_hbm, group + 2, idx_bufs[slot], sem, slot).start()
        return carry

    lax.fori_loop(0, groups_per_step // 2, pair, 0)


def _peer_u_kernel(tiles_a, e_hbm, xa_ref, xb_ref, g_ref, tab_ref, w_ref, x_ref, xi_ref, a2_ref,
                   tile_ref, idx_a, idx_b, sem):
    lane = lax.broadcasted_iota(I32, (2 * PEER_GROUP, PEER_PICKS), 1)
    x_ref[...] = _two_part_tile(tiles_a, xa_ref, xb_ref)
    half = D_MODEL // 2

    def process(idx_ref, local):
        t0 = pl.multiple_of(local * PEER_GROUP, PEER_GROUP)
        _gather_group(idx_ref, tab_ref, tile_ref)
        xg = x_ref[pl.ds(t0, PEER_GROUP), :]
        for c in range(PACK_CHUNKS):
            _interleave_store(xi_ref.at[c], xg[:, c * LANES:(c + 1) * LANES],
                              xg[:, half + c * LANES:half + (c + 1) * LANES])
        xi = [xi_ref[c] for c in range(PACK_CHUNKS)]
        a2 = jnp.zeros((2 * PEER_GROUP, PEER_PICKS), F32)
        for j in range(PEER_PICKS):
            p = None
            for c in range(PACK_CHUNKS):
                term = _unpack(_tile_rows(tile_ref, c, j)) * xi[c]
                p = term if p is None else p + term
            a2 = jnp.where(lane == j, jnp.sum(p, axis=1, keepdims=True), a2)
        a2_ref[...] = a2
        a = a2_ref[pl.ds(0, PEER_GROUP, stride=2), :] + a2_ref[pl.ds(1, PEER_GROUP, stride=2), :]
        w_ref[pl.ds(t0, PEER_GROUP), :] = g_ref[pl.ds(t0, PEER_GROUP), :] * jax.nn.gelu(a)

    _for_each_group(e_hbm, (idx_a, idx_b), sem, x_ref.shape[0] // PEER_GROUP, process)


def _peer_v_kernel(e_hbm, w_ref, tab_ref, f_ref, w2_ref, acc_ref, tile_ref, idx_a, idx_b, sem):
    half = D_MODEL // 2

    def process(idx_ref, local):
        t0 = pl.multiple_of(local * PEER_GROUP, PEER_GROUP)
        _gather_group(idx_ref, tab_ref, tile_ref)
        wg = w_ref[pl.ds(t0, PEER_GROUP), :]
        _interleave_store(w2_ref, wg, wg)
        w2 = w2_ref[...]
        acc = [jnp.zeros((2 * PEER_GROUP, LANES), F32) for _ in range(PACK_CHUNKS)]
        for j in range(PEER_PICKS):
            wb = jnp.broadcast_to(w2[:, j:j + 1], (2 * PEER_GROUP, LANES))
            for c in range(PACK_CHUNKS):
                acc[c] = acc[c] + wb * _unpack(_tile_rows(tile_ref, c, j))
        for c in range(PACK_CHUNKS):
            acc_ref[c] = acc[c]
            f_ref[pl.ds(t0, PEER_GROUP), c * LANES:(c + 1) * LANES] = (
                acc_ref[c, pl.ds(0, PEER_GROUP, stride=2), :])
            f_ref[pl.ds(t0, PEER_GROUP), half + c * LANES:half + (c + 1) * LANES] = (
                acc_ref[c, pl.ds(1, PEER_GROUP, stride=2), :])

    _for_each_group(e_hbm, (idx_a, idx_b), sem, w_ref.shape[0] // PEER_GROUP, process)


def _peer_specs(tb):
    assert (tb // PEER_GROUP) % 2 == 0
    idx = pl.BlockSpec(memory_space=pl.ANY)
    picks = pl.BlockSpec((tb, PEER_PICKS), lambda i: (i, 0))
    feat = pl.BlockSpec((tb, D_MODEL), lambda i: (i, 0))
    table = pl.BlockSpec((N_EXPERTS * PACK_CHUNKS, LANES), lambda i: (0, 0),
                         pipeline_mode=pl.Buffered(1))
    scratch = [pltpu.VMEM((PACK_CHUNKS * PEER_ROWS, LANES), I32),
               pltpu.SMEM((PEER_ROWS,), I32), pltpu.SMEM((PEER_ROWS,), I32),
               pltpu.SemaphoreType.DMA((2,))]
    params = pltpu.CompilerParams(dimension_semantics=("arbitrary",),
                                  vmem_limit_bytes=48 * MIB)
    return idx, picks, feat, table, scratch, params


def _peer_u(e_grp, x1a, x1b, g, tab_u, tb):
    T = x1a.shape[0] + x1b.shape[0]
    tiles_a = x1a.shape[0] // tb
    idx, picks, feat, table, scratch, params = _peer_specs(tb)
    return pl.pallas_call(
        functools.partial(_peer_u_kernel, tiles_a), grid=(T // tb,),
        in_specs=[idx, *_two_part_specs(tiles_a, tb, D_MODEL), picks, table], out_specs=picks,
        out_shape=jax.ShapeDtypeStruct((T, PEER_PICKS), F32),
        scratch_shapes=[pltpu.VMEM((tb, D_MODEL), F32),
                        pltpu.VMEM((PACK_CHUNKS, 2 * PEER_GROUP, LANES), F32),
                        pltpu.VMEM((2 * PEER_GROUP, PEER_PICKS), F32)] + scratch,
        compiler_params=params,
        name="peer_u",
    )(e_grp, x1a, x1b, g, tab_u)


def _peer_v(e_grp, w, tab_v, tb):
    T = w.shape[0]
    idx, picks, feat, table, scratch, params = _peer_specs(tb)
    return pl.pallas_call(
        _peer_v_kernel, grid=(T // tb,),
        in_specs=[idx, picks, table], out_specs=feat,
        out_shape=jax.ShapeDtypeStruct((T, D_MODEL), F32),
        scratch_shapes=[pltpu.VMEM((2 * PEER_GROUP, PEER_PICKS), F32),
                        pltpu.VMEM((PACK_CHUNKS, 2 * PEER_GROUP, LANES), F32)] + scratch,
        compiler_params=params, name="peer_v",
    )(e_grp, w, tab_v)


def _final_kernel(x1_ref, f_ref, p_ref, wg_ref, wp_ref, g_ref, b_ref, y_ref):
    x1 = x1_ref[...]
    e = jax.nn.sigmoid(_dot(x1, wg_ref[...])) * _dot(p_ref[...], wp_ref[...])
    y_ref[...] = _layer_norm(ALPHA * x1 + f_ref[...] + e, g_ref[...], b_ref[...])


def _final(x1, f, f_row_offset, p, wg, wp, ln_g, ln_b, tm):
    T = p.shape[0]
    off = f_row_offset // tm
    full = lambda a: pl.BlockSpec(a.shape, lambda i: (0,) * a.ndim)
    shifted = pl.BlockSpec((tm, D_MODEL), lambda i: (i + off, 0))
    return pl.pallas_call(
        _final_kernel,
        grid=(T // tm,),
        in_specs=[pl.BlockSpec((tm, D_MODEL), lambda i: (i, 0)), shifted,
                  pl.BlockSpec((tm, PLE_DIM), lambda i: (i, 0)),
                  full(wg), full(wp), full(ln_g), full(ln_b)],
        out_specs=pl.BlockSpec((tm, D_MODEL), lambda i: (i, 0)),
        out_shape=jax.ShapeDtypeStruct((T, D_MODEL), F32),
        compiler_params=pltpu.CompilerParams(
            dimension_semantics=("arbitrary",), vmem_limit_bytes=48 * MIB),
        name="final",
    )(x1, f, p, wg, wp, ln_g, ln_b)


def _t5_bias_by_distance(dist, table):
    n = jnp.maximum(dist, 0)
    max_exact = N_BUCKETS // 2
    nf = jnp.maximum(n, 1).astype(F32)
    large = max_exact + jnp.floor(jnp.log(nf / max_exact) / math.log(MAX_DISTANCE / max_exact)
                                  * (N_BUCKETS - max_exact)).astype(I32)
    large = jnp.minimum(large, N_BUCKETS - 1)
    bucket = jnp.where(n < max_exact, n, large)
    hit = bucket[None, :, None] == jnp.arange(N_BUCKETS)
    return jnp.sum(jnp.where(hit, table.T[:, None, :], 0.0), axis=-1)


def _band_bias(table, n_query, n_key, key_offset):
    d_max = n_query - 1 + key_offset
    d_min = key_offset - (n_key - 1)
    by_dist = _t5_bias_by_distance(jnp.arange(d_max, d_min - 1, -1), table)
    rows = [by_dist[:, d_max - (q + key_offset):d_max - (q + key_offset) + n_key]
            for q in range(n_query)]
    return jnp.stack(rows, axis=1)


def kernel(x_prompt, x_sample, cache_k_win, cache_v_win, p_prompt, p_sample, rel_bias_table,
           w_in, attn_sinks, w_att_out, c_ln_g, c_ln_b, c_ws, c_bs, w_chunk_out, w_o, ln1_g,
           ln1_b, peer_wq, peer_subkeys, peer_u, peer_v, w_ple_gate, w_ple_proj, ln2_g, ln2_b):
    batch, seq, d = x_prompt.shape
    n_seq, s_new, _ = x_sample.shape
    w_buf = cache_k_win.shape[2]
    assert d == D_MODEL and w_in.shape[0] == DEPTH and seq % ATT_BLOCK == 0
    tp = batch * seq
    ts = n_seq * s_new
    t_all = tp + ts
    nb = seq // ATT_BLOCK
    seqs_per_step = 8
    assert n_seq % seqs_per_step == 0 and tp % (seqs_per_step * s_new) == 0
    assert tp % 256 == 0 and ts % 256 == 0 and tp % PEER_TOKENS_PER_STEP == 0

    row2 = lambda v: v.reshape(1, -1)
    w_in_b = w_in[0].astype(BF16)
    wa, wc, wo = w_att_out[0].astype(BF16), w_chunk_out[0].astype(BF16), w_o[0].astype(BF16)
    sinks = attn_sinks[0]
    cg, cb = row2(c_ln_g[0]), row2(c_ln_b[0])
    l1g, l1b = row2(ln1_g[0]), row2(ln1_b[0])
    l2g, l2b = row2(ln2_g[0]), row2(ln2_b[0])
    ws, bs = c_ws[0], c_bs[0]

    bias_p = _band_bias(rel_bias_table, ATT_BLOCK, 2 * ATT_BLOCK, ATT_BLOCK)
    bias_s = _band_bias(rel_bias_table, s_new, w_buf + s_new, w_buf)
    bias_s = bias_s.reshape(N_KV, REP * s_new, w_buf + s_new)
    bias_sc, bias_sn = bias_s[:, :, :w_buf], bias_s[:, :, w_buf:]

    bs_exp_p = jnp.repeat(bs.T, C_GDIM, axis=1)
    bs_exp_s = jnp.tile(jnp.repeat(bs[:, :s_new].T, C_GDIM, axis=1), (seqs_per_step, 1))
    tril = jnp.tril(jnp.ones((s_new, s_new), F32))
    eye = jnp.eye(seqs_per_step, dtype=F32)
    wbd = jnp.stack([jnp.kron(eye, ws[g, :s_new, :s_new] * tril) for g in range(C_GROUPS)])

    xp = x_prompt.reshape(tp, d)
    q, kv, cu, cvn, gate = _inproj(xp, w_in_b, cg, cb, 256)
    x1p = _prompt_blocks(nb, sinks, q, kv, cu, cvn, gate, xp, bias_p, ws, bs_exp_p,
                        wa, wc, wo, l1g, l1b)
    kv_tail = kv.reshape(batch, seq, 2 * KV_WIDTH)[:, seq - w_buf:]
    kp = kv_tail[..., :KV_WIDTH].reshape(1, batch, w_buf, N_KV, HEAD_DIM)
    vp = kv_tail[..., KV_WIDTH:].reshape(1, batch, w_buf, N_KV, HEAD_DIM)

    xs = x_sample.reshape(ts, d)
    q, kv, cu, cvn_s, gate = _inproj(xs, w_in_b, cg, cb, 256)
    x1s, k_new, v_new = _sample_blocks(
        seqs_per_step, s_new, sinks, q, kv, cu, cvn_s, gate, xs,
        cache_k_win[0].reshape(n_seq, w_buf, KV_WIDTH),
        cache_v_win[0].reshape(n_seq, w_buf, KV_WIDTH),
        bias_sc, bias_sn, wbd, bs_exp_s, wa, wc, wo, l1g, l1b)
    ks_out = k_new.reshape(1, n_seq, w_buf, N_KV, HEAD_DIM)
    vs_out = v_new.reshape(1, n_seq, w_buf, N_KV, HEAD_DIM)
    cs_out = cvn_s.reshape(1, n_seq, s_new, C_WIDTH)

    e_t, g_t = _route(x1p, x1s, peer_wq[0].astype(BF16), peer_subkeys[0], ROUTE_TILE)
    nt = t_all // ROUTE_TILE
    e_grp = (e_t.reshape(nt, PEER_PICKS, ROUTE_TILE // PEER_GROUP, PEER_GROUP)
             .transpose(0, 2, 1, 3).reshape(t_all // PEER_GROUP, PEER_ROWS))
    g_tok = g_t.transpose(0, 2, 1).reshape(t_all, PEER_PICKS)
    tab_u = _pack_table(peer_u[0], 256)
    tab_v = _pack_table(peer_v[0], 256)
    w_tok = _peer_u(e_grp, x1p, x1s, g_tok, tab_u, PEER_TOKENS_PER_STEP)
    f = _peer_v(e_grp, w_tok, tab_v, PEER_TOKENS_PER_STEP)

    wg, wp = w_ple_gate[0].astype(BF16), w_ple_proj[0].astype(BF16)
    yp = _final(x1p, f, 0, p_prompt[0].reshape(tp, PLE_DIM), wg, wp, l2g, l2b, 256)
    ys = _final(x1s, f, tp, p_sample[0].reshape(ts, PLE_DIM), wg, wp, l2g, l2b, 256)
    return (yp.reshape(batch, seq, d), ys.reshape(n_seq, s_new, d), kp, vp, ks_out, vs_out,
            cs_out)
```

```python
import functools
import math

import jax
import jax.numpy as jnp
from jax import lax
from jax.experimental import pallas as pl
from jax.experimental.pallas import tpu as pltpu

F32 = jnp.float32
BF16 = jnp.bfloat16
I32 = jnp.int32

D_MODEL = 1024
N_HEADS = 8
N_KV = 2
REP = N_HEADS // N_KV
HEAD_DIM = 64
Q_WIDTH = N_HEADS * HEAD_DIM
KV_WIDTH = N_KV * HEAD_DIM
WINDOW = 128
ATT_BLOCK = 128
N_BUCKETS = 32
MAX_DISTANCE = 128
C_GROUPS = 4
C_CHUNK = 128
C_WIDTH = 512
C_GDIM = C_WIDTH // C_GROUPS
PLE_DIM = 256
PEER_HEADS = 8
N_KEYS = 128
N_EXPERTS = N_KEYS * N_KEYS
PEER_TOPK = 16
PEER_DKEY = 256
PEER_DHALF = PEER_DKEY // 2
PEER_PICKS = PEER_HEADS * PEER_TOPK
DEPTH = 1
ALPHA = (2.0 * DEPTH) ** 0.25
LN_EPS = 1e-5
NEG = -1e30
IN_WIDTH = Q_WIDTH + 2 * KV_WIDTH + 2 * C_WIDTH + 2 * D_MODEL

LANES = 128
SUBLANES = 8
MIB = 1024 * 1024

PEER_GROUP = SUBLANES
PEER_ROWS = PEER_GROUP * PEER_PICKS
PACK_CHUNKS = D_MODEL // (2 * LANES)
PEER_TOKENS_PER_STEP = 256
ROUTE_TILE = 512


def _layer_norm(x, g, b):
    mu = jnp.mean(x, axis=-1, keepdims=True)
    xc = x - mu
    var = jnp.mean(xc * xc, axis=-1, keepdims=True)
    return xc * lax.rsqrt(var + LN_EPS) * g + b


def _dot(a, b):
    return jnp.dot(a.astype(BF16), b.astype(BF16), preferred_element_type=F32)


def _dot_nt(a, b):
    return lax.dot_general(a.astype(BF16), b.astype(BF16), (((1,), (1,)), ((), ())),
                           preferred_element_type=F32)


def _inproj_kernel(x_ref, w_ref, g_ref, b_ref, q_ref, kv_ref, cu_ref, cvn_ref, gate_ref):
    z = jnp.dot(x_ref[...].astype(BF16), w_ref[...], preferred_element_type=F32)
    o = 0
    q_ref[...] = z[:, o:o + Q_WIDTH]
    o += Q_WIDTH
    kv_ref[...] = z[:, o:o + 2 * KV_WIDTH]
    o += 2 * KV_WIDTH
    cu_ref[...] = z[:, o:o + C_WIDTH]
    o += C_WIDTH
    cvn_ref[...] = _layer_norm(z[:, o:o + C_WIDTH], g_ref[...], b_ref[...])
    o += C_WIDTH
    gate_ref[...] = z[:, o:o + 2 * D_MODEL]


def _inproj(x, w_bf16, c_g, c_b, tm):
    T = x.shape[0]
    widths = (Q_WIDTH, 2 * KV_WIDTH, C_WIDTH, C_WIDTH, 2 * D_MODEL)
    row = lambda n: pl.BlockSpec((tm, n), lambda i: (i, 0))
    full = lambda a: pl.BlockSpec(a.shape, lambda i: (0,) * a.ndim)
    return pl.pallas_call(
        _inproj_kernel,
        grid=(T // tm,),
        in_specs=[row(D_MODEL), full(w_bf16), full(c_g), full(c_b)],
        out_specs=[row(n) for n in widths],
        out_shape=[jax.ShapeDtypeStruct((T, n), F32) for n in widths],
        compiler_params=pltpu.CompilerParams(
            dimension_semantics=("arbitrary",), vmem_limit_bytes=48 * MIB),
        name="inproj",
    )(x, w_bf16, c_g, c_b)


def _merge_and_norm(x, ya, yc, gate, wa_ref, wc_ref, wo_ref, g_ref, b_ref):
    ga = gate[:, :D_MODEL]
    gc = gate[:, D_MODEL:]
    mix = jax.nn.sigmoid(ga) * _dot(ya, wa_ref[...]) + jax.nn.sigmoid(gc) * _dot(yc, wc_ref[...])
    h = ALPHA * x + _dot(mix, wo_ref[...])
    return _layer_norm(h, g_ref[...], b_ref[...])


def _softmax_with_sink(s, sink):
    m = jnp.maximum(jnp.max(s, axis=-1, keepdims=True), sink)
    e = jnp.exp(s - m)
    return e / (jnp.sum(e, axis=-1, keepdims=True) + jnp.exp(sink - m))


PROMPT_BLOCKS_PER_STEP = 2


def _prompt_block_kernel(nb, sinks_ref, q_ref, kvp_ref, kvo_ref, cu_ref, cvn_ref, gate_ref,
                         x_ref, bias_ref, ws_ref, bs_ref, wa_ref, wc_ref, wo_ref, g_ref,
                         b_ref, x1_ref):
    blk = ATT_BLOCK
    qi = lax.broadcasted_iota(I32, (blk, 2 * blk), 0)
    kj = lax.broadcasted_iota(I32, (blk, 2 * blk), 1)
    dist = qi - kj + blk
    in_window = (dist >= 0) & (dist < WINDOW)
    ti = lax.broadcasted_iota(I32, (C_CHUNK, C_CHUNK), 0)
    si = lax.broadcasted_iota(I32, (C_CHUNK, C_CHUNK), 1)
    wms = [jnp.where(si <= ti, ws_ref[g], 0.0) for g in range(C_GROUPS)]

    scale = HEAD_DIM ** -0.5
    work = [(sb, h) for sb in range(PROMPT_BLOCKS_PER_STEP) for h in range(N_HEADS)]
    kvs, scores = {}, {}
    for sb in range(PROMPT_BLOCKS_PER_STEP):
        rows = slice(sb * blk, (sb + 1) * blk)
        kv_prev = kvp_ref[...] if sb == 0 else kvo_ref[(sb - 1) * blk:sb * blk, :]
        kvs[sb] = jnp.concatenate([kv_prev, kvo_ref[rows, :]], axis=0)
    for sb, h in work:
        n = lax.rem(pl.program_id(0) * PROMPT_BLOCKS_PER_STEP + sb, nb)
        mask = in_window & ((kj >= blk) | (n > 0))
        g = h // REP
        qh = q_ref[sb * blk:(sb + 1) * blk, h * HEAD_DIM:(h + 1) * HEAD_DIM]
        kg = kvs[sb][:, g * HEAD_DIM:(g + 1) * HEAD_DIM]
        scores[sb, h] = jnp.where(mask, _dot_nt(qh, kg) * scale + bias_ref[h], NEG)
    probs = {key: _softmax_with_sink(scores[key], sinks_ref[key[1]]) for key in work}
    ya_blocks, yc_blocks = [], []
    for sb in range(PROMPT_BLOCKS_PER_STEP):
        rows = slice(sb * blk, (sb + 1) * blk)
        outs = []
        for h in range(N_HEADS):
            g = h // REP
            vg = kvs[sb][:, KV_WIDTH + g * HEAD_DIM:KV_WIDTH + (g + 1) * HEAD_DIM]
            outs.append(_dot(probs[sb, h], vg))
        ya_blocks.append(jnp.concatenate(outs, axis=1))

        cvn = cvn_ref[rows, :]
        parts = [_dot(wms[g], cvn[:, g * C_GDIM:(g + 1) * C_GDIM]) for g in range(C_GROUPS)]
        yc_blocks.append(cu_ref[rows, :] * (jnp.concatenate(parts, axis=1) + bs_ref[...]))

    ya = jnp.concatenate(ya_blocks, axis=0)
    yc = jnp.concatenate(yc_blocks, axis=0)
    x1_ref[...] = _merge_and_norm(x_ref[...], ya, yc, gate_ref[...], wa_ref, wc_ref, wo_ref,
                                  g_ref, b_ref)


def _prompt_blocks(nb, sinks, q, kv, cu, cvn, gate, x, bias, ws, bs_exp,
                   wa, wc, wo, ln_g, ln_b):
    T = q.shape[0]
    blk = ATT_BLOCK
    tm = PROMPT_BLOCKS_PER_STEP * blk
    assert nb % PROMPT_BLOCKS_PER_STEP == 0
    row = lambda n: pl.BlockSpec((tm, n), lambda i: (i, 0))
    full = lambda a: pl.BlockSpec(a.shape, lambda i: (0,) * a.ndim)
    prev = pl.BlockSpec((blk, 2 * KV_WIDTH),
                        lambda i: (jnp.maximum(i * PROMPT_BLOCKS_PER_STEP - 1, 0), 0))
    return pl.pallas_call(
        functools.partial(_prompt_block_kernel, nb),
        grid=(T // tm,),
        in_specs=[pl.BlockSpec(memory_space=pltpu.SMEM),
                  row(Q_WIDTH), prev, row(2 * KV_WIDTH), row(C_WIDTH), row(C_WIDTH),
                  row(2 * D_MODEL), row(D_MODEL), full(bias), full(ws), full(bs_exp),
                  full(wa), full(wc), full(wo), full(ln_g), full(ln_b)],
        out_specs=row(D_MODEL),
        out_shape=jax.ShapeDtypeStruct((T, D_MODEL), F32),
        compiler_params=pltpu.CompilerParams(
            dimension_semantics=("arbitrary",), vmem_limit_bytes=48 * MIB),
        name="prompt_blocks",
    )(sinks, q, kv, kv, cu, cvn, gate, x, bias, ws, bs_exp, wa, wc, wo, ln_g, ln_b)


def _sample_block_kernel(seqs, s_new, sinks_ref, q_ref, kv_ref, cu_ref, cvn_ref, gate_ref,
                         x_ref, ck_ref, cv_ref, biasc_ref, biasn_ref, wbd_ref, bs_ref, wa_ref,
                         wc_ref, wo_ref, g_ref, b_ref, x1_ref, kout_ref, vout_ref):
    w_buf = ck_ref.shape[1]
    rows = REP * s_new
    qi_c = lax.rem(lax.broadcasted_iota(I32, (rows, w_buf), 0), s_new)
    kj_c = lax.broadcasted_iota(I32, (rows, w_buf), 1)
    mask_c = (qi_c + w_buf - kj_c) < WINDOW
    qi_n = lax.rem(lax.broadcasted_iota(I32, (rows, s_new), 0), s_new)
    kj_n = lax.broadcasted_iota(I32, (rows, s_new), 1)
    mask_n = kj_n <= qi_n
    sink_col = [jnp.concatenate([jnp.full((s_new, 1), sinks_ref[g * REP + r], F32)
                                 for r in range(REP)], axis=0) for g in range(N_KV)]

    q_all = q_ref[...]
    kv_all = kv_ref[...]
    scale = HEAD_DIM ** -0.5
    work = [(b, g) for b in range(seqs) for g in range(N_KV)]
    ck, cv, kvb, scores = {}, {}, {}, {}
    for b in range(seqs):
        kvb[b] = kv_all[b * s_new:(b + 1) * s_new]
        ck[b] = ck_ref[b]
        cv[b] = cv_ref[b]
        kout_ref[b] = jnp.concatenate([ck[b][s_new:], kvb[b][:, :KV_WIDTH]], axis=0)
        vout_ref[b] = jnp.concatenate([cv[b][s_new:], kvb[b][:, KV_WIDTH:]], axis=0)
    for b, g in work:
        qb = q_all[b * s_new:(b + 1) * s_new]
        qg = jnp.concatenate([qb[:, (g * REP + r) * HEAD_DIM:(g * REP + r + 1) * HEAD_DIM]
                              for r in range(REP)], axis=0)
        lane = slice(g * HEAD_DIM, (g + 1) * HEAD_DIM)
        sc = _dot_nt(qg, ck[b][:, lane]) * scale + biasc_ref[g]
        sn = _dot_nt(qg, kvb[b][:, lane]) * scale + biasn_ref[g]
        scores[b, g] = (jnp.where(mask_c, sc, NEG), jnp.where(mask_n, sn, NEG))
    probs = {}
    for b, g in work:
        sc, sn = scores[b, g]
        sink = sink_col[g]
        m = jnp.maximum(jnp.maximum(jnp.max(sc, axis=-1, keepdims=True),
                                    jnp.max(sn, axis=-1, keepdims=True)), sink)
        ec = jnp.exp(sc - m)
        en = jnp.exp(sn - m)
        den = (jnp.sum(ec, axis=-1, keepdims=True) + jnp.sum(en, axis=-1, keepdims=True)
               + jnp.exp(sink - m))
        probs[b, g] = (ec / den, en / den)
    ya_rows = []
    for b in range(seqs):
        heads = []
        for g in range(N_KV):
            lane = slice(g * HEAD_DIM, (g + 1) * HEAD_DIM)
            vlane = slice(KV_WIDTH + g * HEAD_DIM, KV_WIDTH + (g + 1) * HEAD_DIM)
            pc, pn = probs[b, g]
            og = _dot(pc, cv[b][:, lane]) + _dot(pn, kvb[b][:, vlane])
            heads.extend(og[r * s_new:(r + 1) * s_new] for r in range(REP))
        ya_rows.append(jnp.concatenate(heads, axis=1))
    ya = jnp.concatenate(ya_rows, axis=0)

    cvn = cvn_ref[...]
    parts = [_dot(wbd_ref[g], cvn[:, g * C_GDIM:(g + 1) * C_GDIM]) for g in range(C_GROUPS)]
    yc = cu_ref[...] * (jnp.concatenate(parts, axis=1) + bs_ref[...])

    x1_ref[...] = _merge_and_norm(x_ref[...], ya, yc, gate_ref[...], wa_ref, wc_ref, wo_ref,
                                  g_ref, b_ref)


def _sample_blocks(seqs, s_new, sinks, q, kv, cu, cvn, gate, x, cache_k,
                   cache_v, bias_c, bias_n, wbd, bs_exp, wa, wc, wo, ln_g, ln_b):
    T = q.shape[0]
    n_seq, w_buf, kvw = cache_k.shape
    tm = seqs * s_new
    row = lambda n: pl.BlockSpec((tm, n), lambda i: (i, 0))
    full = lambda a: pl.BlockSpec(a.shape, lambda i: (0,) * a.ndim)
    cache = pl.BlockSpec((seqs, w_buf, kvw), lambda i: (i, 0, 0))
    return pl.pallas_call(
        functools.partial(_sample_block_kernel, seqs, s_new),
        grid=(T // tm,),
        in_specs=[pl.BlockSpec(memory_space=pltpu.SMEM),
                  row(Q_WIDTH), row(2 * KV_WIDTH), row(C_WIDTH), row(C_WIDTH),
                  row(2 * D_MODEL), row(D_MODEL), cache, cache, full(bias_c), full(bias_n),
                  full(wbd), full(bs_exp), full(wa), full(wc), full(wo), full(ln_g),
                  full(ln_b)],
        out_specs=[row(D_MODEL), cache, cache],
        out_shape=[jax.ShapeDtypeStruct((T, D_MODEL), F32),
                   jax.ShapeDtypeStruct(cache_k.shape, F32),
                   jax.ShapeDtypeStruct(cache_v.shape, F32)],
        compiler_params=pltpu.CompilerParams(
            dimension_semantics=("arbitrary",), vmem_limit_bytes=48 * MIB),
        name="sample_blocks",
    )(sinks, q, kv, cu, cvn, gate, x, cache_k, cache_v, bias_c, bias_n, wbd, bs_exp, wa, wc,
      wo, ln_g, ln_b)


def _two_part_specs(tiles_a, tile_rows, width):
    first = pl.BlockSpec((tile_rows, width), lambda i: (jnp.minimum(i, tiles_a - 1), 0))
    second = pl.BlockSpec((tile_rows, width), lambda i: (jnp.maximum(i - tiles_a, 0), 0))
    return first, second


def _two_part_tile(tiles_a, a_ref, b_ref):
    return jnp.where(pl.program_id(0) < tiles_a, a_ref[...], b_ref[...])


def _top16(s, iota, fill):
    vals, idxs = [], []
    for _ in range(PEER_TOPK):
        m = jnp.max(s, axis=0, keepdims=True)
        i = jnp.min(jnp.where(s == m, iota, fill), axis=0, keepdims=True)
        vals.append(m)
        idxs.append(i)
        s = jnp.where(iota == i, -jnp.inf, s)
    return jnp.concatenate(vals, axis=0), jnp.concatenate(idxs, axis=0)


def _pick(table, iota, idx):
    return jnp.concatenate(
        [jnp.sum(jnp.where(iota == idx[r:r + 1], table, 0), axis=0, keepdims=True)
         for r in range(PEER_TOPK)], axis=0)


_CAND_SHORT = SUBLANES


def _cand_flat_index(tt):
    i16 = lax.broadcasted_iota(I32, (PEER_TOPK, tt), 0)
    i8 = lax.broadcasted_iota(I32, (_CAND_SHORT, tt), 0)
    pieces = [i16] + [k * PEER_TOPK + i8 for k in range(1, _CAND_SHORT)]
    pieces.append((_CAND_SHORT + i8) * PEER_TOPK)
    return jnp.concatenate(pieces, axis=0)


def _cand_values(sv0, sv1):
    pieces = [sv0[0:1] + sv1]
    pieces += [sv0[k:k + 1] + sv1[0:_CAND_SHORT] for k in range(1, _CAND_SHORT)]
    pieces.append(sv0[_CAND_SHORT:] + sv1[0:1])
    return jnp.concatenate(pieces, axis=0)


def _route_kernel(tiles_a, xa_ref, xb_ref, wq_ref, sk_ref, e_ref, g_ref, q_scr):
    tt = xa_ref.shape[0]
    x1 = _two_part_tile(tiles_a, xa_ref, xb_ref)
    qf = jnp.dot(x1.astype(BF16), wq_ref[...], preferred_element_type=F32)
    for hc in range(2 * PEER_HEADS):
        q_scr[hc] = qf[:, hc * PEER_DHALF:(hc + 1) * PEER_DHALF]
    iota_n = lax.broadcasted_iota(I32, (N_KEYS, tt), 0)
    iota_k = lax.broadcasted_iota(I32, (PEER_TOPK, tt), 0)
    flat = _cand_flat_index(tt)

    def first_stage(h):
        out = []
        for c in range(2):
            s = _dot_nt(sk_ref[h, c], q_scr[2 * h + c])
            out.extend(_top16(s, iota_n, N_KEYS))
        return out[0], out[2], out[1], out[3]

    def second_stage(h, carry):
        sv0, sv1, si0, si1 = carry
        fv, fi = _top16(_cand_values(sv0, sv1), flat, PEER_TOPK * PEER_TOPK)
        i1 = _pick(si0, iota_k, lax.shift_right_logical(fi, 4))
        i2 = _pick(si1, iota_k, lax.bitwise_and(fi, PEER_TOPK - 1))
        ex = jnp.exp(fv - fv[0:1])
        row0 = pl.multiple_of(h * PEER_TOPK, PEER_TOPK)
        e_ref[0, pl.ds(row0, PEER_TOPK), :] = (i1 * N_KEYS + i2) * PACK_CHUNKS
        g_ref[0, pl.ds(row0, PEER_TOPK), :] = ex / jnp.sum(ex, axis=0, keepdims=True)

    def body(h, carry):
        second_stage(h - 1, carry)
        return first_stage(h)

    last = lax.fori_loop(1, PEER_HEADS, body, first_stage(0))
    second_stage(PEER_HEADS - 1, last)


def _route(x1a, x1b, wq_bf16, subkeys, tt):
    T = x1a.shape[0] + x1b.shape[0]
    tiles_a = x1a.shape[0] // tt
    full = lambda a: pl.BlockSpec(a.shape, lambda i: (0,) * a.ndim)
    out = pl.BlockSpec((1, PEER_PICKS, tt), lambda i: (i, 0, 0))
    return pl.pallas_call(
        functools.partial(_route_kernel, tiles_a),
        grid=(T // tt,),
        in_specs=[*_two_part_specs(tiles_a, tt, D_MODEL), full(wq_bf16), full(subkeys)],
        out_specs=[out, out],
        out_shape=[jax.ShapeDtypeStruct((T // tt, PEER_PICKS, tt), I32),
                   jax.ShapeDtypeStruct((T // tt, PEER_PICKS, tt), F32)],
        scratch_shapes=[pltpu.VMEM((2 * PEER_HEADS, tt, PEER_DHALF), F32)],
        compiler_params=pltpu.CompilerParams(
            dimension_semantics=("arbitrary",), vmem_limit_bytes=48 * MIB),
        name="peer_route",
    )(x1a, x1b, wq_bf16, subkeys)


def _pack_kernel(t_ref, o_ref, rows_ref):
    tm = t_ref.shape[0]
    half = D_MODEL // 2
    for c in range(PACK_CHUNKS):
        rows_ref[pl.ds(2 * c, tm, stride=2 * PACK_CHUNKS), :] = t_ref[:, c * LANES:(c + 1) * LANES]
        rows_ref[pl.ds(2 * c + 1, tm, stride=2 * PACK_CHUNKS), :] = (
            t_ref[:, half + c * LANES:half + (c + 1) * LANES])
    o_ref[...] = pltpu.bitcast(rows_ref[...].astype(BF16), I32)


def _pack_table(tab, tm):
    n = tab.shape[0]
    return pl.pallas_call(
        _pack_kernel,
        grid=(n // tm,),
        in_specs=[pl.BlockSpec((tm, D_MODEL), lambda i: (i, 0))],
        out_specs=pl.BlockSpec((tm * PACK_CHUNKS, LANES), lambda i: (i, 0)),
        out_shape=jax.ShapeDtypeStruct((n * PACK_CHUNKS, LANES), I32),
        scratch_shapes=[pltpu.VMEM((2 * tm * PACK_CHUNKS, LANES), F32)],
        compiler_params=pltpu.CompilerParams(dimension_semantics=("arbitrary",)),
        name="pack_table",
    )(tab)


def _unpack(word):
    return pltpu.bitcast(word, BF16).astype(F32)


def _interleave_store(ref, even_rows, odd_rows):
    n = even_rows.shape[0]
    ref[pl.ds(0, n, stride=2), :] = even_rows
    ref[pl.ds(1, n, stride=2), :] = odd_rows


def _gather_group(idx_ref, tab_ref, tile_ref):
    for r in range(PEER_ROWS):
        row = pl.multiple_of(idx_ref[r], PACK_CHUNKS)
        tile_ref[pl.ds(r * PACK_CHUNKS, PACK_CHUNKS), :] = tab_ref[pl.ds(row, PACK_CHUNKS), :]


def _tile_rows(tile_ref, c, j):
    start = j * PEER_GROUP * PACK_CHUNKS + c
    return tile_ref[pl.ds(start, PEER_GROUP, stride=PACK_CHUNKS), :]


def _index_copy(e_hbm, group, buf, sem, slot):
    return pltpu.make_async_copy(e_hbm.at[group], buf, sem.at[slot])


def _for_each_group(e_hbm, idx_bufs, sem, groups_per_step, process):
    step = pl.program_id(0)
    total = pl.num_programs(0) * groups_per_step
    base = step * groups_per_step

    @pl.when(step == 0)
    def _():
        for slot in range(2):
            _index_copy(e_hbm, slot, idx_bufs[slot], sem, slot).start()

    def pair(p, carry):
        for slot in range(2):
            local = 2 * p + slot
            group = base + local
            _index_copy(e_hbm, group, idx_bufs[slot], sem, slot).wait()
            process(idx_bufs[slot], local)

            @pl.when(group + 2 < total)
            def _():
                _index_copy(e_hbm, group + 2, idx_bufs[slot], sem, slot).start()
        return carry

    lax.fori_loop(0, groups_per_step // 2, pair, 0)


def _peer_u_kernel(tiles_a, e_hbm, xa_ref, xb_ref, g_ref, tab_ref, w_ref, x_ref, xi_ref, a2_ref,
                   tile_ref, idx_a, idx_b, sem):
    lane = lax.broadcasted_iota(I32, (2 * PEER_GROUP, PEER_PICKS), 1)
    x_ref[...] = _two_part_tile(tiles_a, xa_ref, xb_ref)
    half = D_MODEL // 2

    def process(idx_ref, local):
        t0 = pl.multiple_of(local * PEER_GROUP, PEER_GROUP)
        _gather_group(idx_ref, tab_ref, tile_ref)
        xg = x_ref[pl.ds(t0, PEER_GROUP), :]
        for c in range(PACK_CHUNKS):
            _interleave_store(xi_ref.at[c], xg[:, c * LANES:(c + 1) * LANES],
                              xg[:, half + c * LANES:half + (c + 1) * LANES])
        xi = [xi_ref[c] for c in range(PACK_CHUNKS)]
        a2 = jnp.zeros((2 * PEER_GROUP, PEER_PICKS), F32)
        for j in range(PEER_PICKS):
            p = None
            for c in range(PACK_CHUNKS):
                term = _unpack(_tile_rows(tile_ref, c, j)) * xi[c]
                p = term if p is None else p + term
            a2 = jnp.where(lane == j, jnp.sum(p, axis=1, keepdims=True), a2)
        a2_ref[...] = a2
        a = a2_ref[pl.ds(0, PEER_GROUP, stride=2), :] + a2_ref[pl.ds(1, PEER_GROUP, stride=2), :]
        w_ref[pl.ds(t0, PEER_GROUP), :] = g_ref[pl.ds(t0, PEER_GROUP), :] * jax.nn.gelu(a)

    _for_each_group(e_hbm, (idx_a, idx_b), sem, x_ref.shape[0] // PEER_GROUP, process)


def _peer_v_kernel(e_hbm, w_ref, tab_ref, f_ref, w2_ref, acc_ref, tile_ref, idx_a, idx_b, sem):
    half = D_MODEL // 2

    def process(idx_ref, local):
        t0 = pl.multiple_of(local * PEER_GROUP, PEER_GROUP)
        _gather_group(idx_ref, tab_ref, tile_ref)
        wg = w_ref[pl.ds(t0, PEER_GROUP), :]
        _interleave_store(w2_ref, wg, wg)
        w2 = w2_ref[...]
        acc = [jnp.zeros((2 * PEER_GROUP, LANES), F32) for _ in range(PACK_CHUNKS)]
        for j in range(PEER_PICKS):
            wb = jnp.broadcast_to(w2[:, j:j + 1], (2 * PEER_GROUP, LANES))
            for c in range(PACK_CHUNKS):
                acc[c] = acc[c] + wb * _unpack(_tile_rows(tile_ref, c, j))
        for c in range(PACK_CHUNKS):
            acc_ref[c] = acc[c]
            f_ref[pl.ds(t0, PEER_GROUP), c * LANES:(c + 1) * LANES] = (
                acc_ref[c, pl.ds(0, PEER_GROUP, stride=2), :])
            f_ref[pl.ds(t0, PEER_GROUP), half + c * LANES:half + (c + 1) * LANES] = (
                acc_ref[c, pl.ds(1, PEER_GROUP, stride=2), :])

    _for_each_group(e_hbm, (idx_a, idx_b), sem, w_ref.shape[0] // PEER_GROUP, process)


def _peer_specs(tb):
    assert (tb // PEER_GROUP) % 2 == 0
    idx = pl.BlockSpec(memory_space=pl.ANY)
    picks = pl.BlockSpec((tb, PEER_PICKS), lambda i: (i, 0))
    feat = pl.BlockSpec((tb, D_MODEL), lambda i: (i, 0))
    table = pl.BlockSpec((N_EXPERTS * PACK_CHUNKS, LANES), lambda i: (0, 0),
                         pipeline_mode=pl.Buffered(1))
    scratch = [pltpu.VMEM((PACK_CHUNKS * PEER_ROWS, LANES), I32),
               pltpu.SMEM((PEER_ROWS,), I32), pltpu.SMEM((PEER_ROWS,), I32),
               pltpu.SemaphoreType.DMA((2,))]
    params = pltpu.CompilerParams(dimension_semantics=("arbitrary",),
                                  vmem_limit_bytes=48 * MIB)
    return idx, picks, feat, table, scratch, params


def _peer_u(e_grp, x1a, x1b, g, tab_u, tb):
    T = x1a.shape[0] + x1b.shape[0]
    tiles_a = x1a.shape[0] // tb
    idx, picks, feat, table, scratch, params = _peer_specs(tb)
    return pl.pallas_call(
        functools.partial(_peer_u_kernel, tiles_a), grid=(T // tb,),
        in_specs=[idx, *_two_part_specs(tiles_a, tb, D_MODEL), picks, table], out_specs=picks,
        out_shape=jax.ShapeDtypeStruct((T, PEER_PICKS), F32),
        scratch_shapes=[pltpu.VMEM((tb, D_MODEL), F32),
                        pltpu.VMEM((PACK_CHUNKS, 2 * PEER_GROUP, LANES), F32),
                        pltpu.VMEM((2 * PEER_GROUP, PEER_PICKS), F32)] + scratch,
        compiler_params=params,
        name="peer_u",
    )(e_grp, x1a, x1b, g, tab_u)


def _peer_v(e_grp, w, tab_v, tb):
    T = w.shape[0]
    idx, picks, feat, table, scratch, params = _peer_specs(tb)
    return pl.pallas_call(
        _peer_v_kernel, grid=(T // tb,),
        in_specs=[idx, picks, table], out_specs=feat,
        out_shape=jax.ShapeDtypeStruct((T, D_MODEL), F32),
        scratch_shapes=[pltpu.VMEM((2 * PEER_GROUP, PEER_PICKS), F32),
                        pltpu.VMEM((PACK_CHUNKS, 2 * PEER_GROUP, LANES), F32)] + scratch,
        compiler_params=params, name="peer_v",
    )(e_grp, w, tab_v)


def _final_kernel(x1_ref, f_ref, p_ref, wg_ref, wp_ref, g_ref, b_ref, y_ref):
    x1 = x1_ref[...]
    e = jax.nn.sigmoid(_dot(x1, wg_ref[...])) * _dot(p_ref[...], wp_ref[...])
    y_ref[...] = _layer_norm(ALPHA * x1 + f_ref[...] + e, g_ref[...], b_ref[...])


def _final(x1, f, f_row_offset, p, wg, wp, ln_g, ln_b, tm):
    T = p.shape[0]
    off = f_row_offset // tm
    full = lambda a: pl.BlockSpec(a.shape, lambda i: (0,) * a.ndim)
    shifted = pl.BlockSpec((tm, D_MODEL), lambda i: (i + off, 0))
    return pl.pallas_call(
        _final_kernel,
        grid=(T // tm,),
        in_specs=[pl.BlockSpec((tm, D_MODEL), lambda i: (i, 0)), shifted,
                  pl.BlockSpec((tm, PLE_DIM), lambda i: (i, 0)),
                  full(wg), full(wp), full(ln_g), full(ln_b)],
        out_specs=pl.BlockSpec((tm, D_MODEL), lambda i: (i, 0)),
        out_shape=jax.ShapeDtypeStruct((T, D_MODEL), F32),
        compiler_params=pltpu.CompilerParams(
            dimension_semantics=("arbitrary",), vmem_limit_bytes=48 * MIB),
        name="final",
    )(x1, f, p, wg, wp, ln_g, ln_b)


def _t5_bias_by_distance(dist, table):
    n = jnp.maximum(dist, 0)
    max_exact = N_BUCKETS // 2
    nf = jnp.maximum(n, 1).astype(F32)
    large = max_exact + jnp.floor(jnp.log(nf / max_exact) / math.log(MAX_DISTANCE / max_exact)
                                  * (N_BUCKETS - max_exact)).astype(I32)
    large = jnp.minimum(large, N_BUCKETS - 1)
    bucket = jnp.where(n < max_exact, n, large)
    hit = bucket[None, :, None] == jnp.arange(N_BUCKETS)
    return jnp.sum(jnp.where(hit, table.T[:, None, :], 0.0), axis=-1)


def _band_bias(table, n_query, n_key, key_offset):
    d_max = n_query - 1 + key_offset
    d_min = key_offset - (n_key - 1)
    by_dist = _t5_bias_by_distance(jnp.arange(d_max, d_min - 1, -1), table)
    rows = [by_dist[:, d_max - (q + key_offset):d_max - (q + key_offset) + n_key]
            for q in range(n_query)]
    return jnp.stack(rows, axis=1)


def kernel(x_prompt, x_sample, cache_k_win, cache_v_win, p_prompt, p_sample, rel_bias_table,
           w_in, attn_sinks, w_att_out, c_ln_g, c_ln_b, c_ws, c_bs, w_chunk_out, w_o, ln1_g,
           ln1_b, peer_wq, peer_subkeys, peer_u, peer_v, w_ple_gate, w_ple_proj, ln2_g, ln2_b):
    batch, seq, d = x_prompt.shape
    n_seq, s_new, _ = x_sample.shape
    w_buf = cache_k_win.shape[2]
    assert d == D_MODEL and w_in.shape[0] == DEPTH and seq % ATT_BLOCK == 0
    tp = batch * seq
    ts = n_seq * s_new
    t_all = tp + ts
    nb = seq // ATT_BLOCK
    seqs_per_step = 8
    assert n_seq % seqs_per_step == 0 and tp % (seqs_per_step * s_new) == 0
    assert tp % 256 == 0 and ts % 256 == 0 and tp % PEER_TOKENS_PER_STEP == 0
    assert tp % ROUTE_TILE == 0 and ts % ROUTE_TILE == 0

    row2 = lambda v: v.reshape(1, -1)
    w_in_b = w_in[0].astype(BF16)
    wa, wc, wo = w_att_out[0].astype(BF16), w_chunk_out[0].astype(BF16), w_o[0].astype(BF16)
    sinks = attn_sinks[0]
    cg, cb = row2(c_ln_g[0]), row2(c_ln_b[0])
    l1g, l1b = row2(ln1_g[0]), row2(ln1_b[0])
    l2g, l2b = row2(ln2_g[0]), row2(ln2_b[0])
    ws, bs = c_ws[0], c_bs[0]

    bias_p = _band_bias(rel_bias_table, ATT_BLOCK, 2 * ATT_BLOCK, ATT_BLOCK)
    bias_s = _band_bias(rel_bias_table, s_new, w_buf + s_new, w_buf)
    bias_s = bias_s.reshape(N_KV, REP * s_new, w_buf + s_new)
    bias_sc, bias_sn = bias_s[:, :, :w_buf], bias_s[:, :, w_buf:]

    bs_exp_p = jnp.repeat(bs.T, C_GDIM, axis=1)
    bs_exp_s = jnp.tile(jnp.repeat(bs[:, :s_new].T, C_GDIM, axis=1), (seqs_per_step, 1))
    tril = jnp.tril(jnp.ones((s_new, s_new), F32))
    eye = jnp.eye(seqs_per_step, dtype=F32)
    wbd = jnp.stack([jnp.kron(eye, ws[g, :s_new, :s_new] * tril) for g in range(C_GROUPS)])

    xp = x_prompt.reshape(tp, d)
    q, kv, cu, cvn, gate = _inproj(xp, w_in_b, cg, cb, 256)
    x1p = _prompt_blocks(nb, sinks, q, kv, cu, cvn, gate, xp, bias_p, ws, bs_exp_p,
                        wa, wc, wo, l1g, l1b)
    kv_tail = kv.reshape(batch, seq, 2 * KV_WIDTH)[:, seq - w_buf:]
    kp = kv_tail[..., :KV_WIDTH].reshape(1, batch, w_buf, N_KV, HEAD_DIM)
    vp = kv_tail[..., KV_WIDTH:].reshape(1, batch, w_buf, N_KV, HEAD_DIM)

    xs = x_sample.reshape(ts, d)
    q, kv, cu, cvn_s, gate = _inproj(xs, w_in_b, cg, cb, 256)
    x1s, k_new, v_new = _sample_blocks(
        seqs_per_step, s_new, sinks, q, kv, cu, cvn_s, gate, xs,
        cache_k_win[0].reshape(n_seq, w_buf, KV_WIDTH),
        cache_v_win[0].reshape(n_seq, w_buf, KV_WIDTH),
        bias_sc, bias_sn, wbd, bs_exp_s, wa, wc, wo, l1g, l1b)
    ks_out = k_new.reshape(1, n_seq, w_buf, N_KV, HEAD_DIM)
    vs_out = v_new.reshape(1, n_seq, w_buf, N_KV, HEAD_DIM)
    cs_out = cvn_s.reshape(1, n_seq, s_new, C_WIDTH)

    e_t, g_t = _route(x1p, x1s, peer_wq[0].astype(BF16), peer_subkeys[0], ROUTE_TILE)
    nt = t_all // ROUTE_TILE
    e_grp = (e_t.reshape(nt, PEER_PICKS, ROUTE_TILE // PEER_GROUP, PEER_GROUP)
             .transpose(0, 2, 1, 3).reshape(t_all // PEER_GROUP, PEER_ROWS))
    g_tok = g_t.transpose(0, 2, 1).reshape(t_all, PEER_PICKS)
    tab_u = _pack_table(peer_u[0], 256)
    tab_v = _pack_table(peer_v[0], 256)
    w_tok = _peer_u(e_grp, x1p, x1s, g_tok, tab_u, PEER_TOKENS_PER_STEP)
    f = _peer_v(e_grp, w_tok, tab_v, PEER_TOKENS_PER_STEP)

    wg, wp = w_ple_gate[0].astype(BF16), w_ple_proj[0].astype(BF16)
    yp = _final(x1p, f, 0, p_prompt[0].reshape(tp, PLE_DIM), wg, wp, l2g, l2b,
                512 if tp % 512 == 0 else 256)
    ys = _final(x1s, f, tp, p_sample[0].reshape(ts, PLE_DIM), wg, wp, l2g, l2b, 256)
    return (yp.reshape(batch, seq, d), ys.reshape(n_seq, s_new, d), kp, vp, ks_out, vs_out,
            cs_out)
```

```python
import functools
import math

import jax
import jax.numpy as jnp
from jax import lax
from jax.experimental import pallas as pl
from jax.experimental.pallas import tpu as pltpu

F32 = jnp.float32
BF16 = jnp.bfloat16
I32 = jnp.int32

D_MODEL = 1024
N_HEADS = 8
N_KV = 2
REP = N_HEADS // N_KV
HEAD_DIM = 64
Q_WIDTH = N_HEADS * HEAD_DIM
KV_WIDTH = N_KV * HEAD_DIM
WINDOW = 128
ATT_BLOCK = 128
N_BUCKETS = 32
MAX_DISTANCE = 128
C_GROUPS = 4
C_CHUNK = 128
C_WIDTH = 512
C_GDIM = C_WIDTH // C_GROUPS
PLE_DIM = 256
PEER_HEADS = 8
N_KEYS = 128
N_EXPERTS = N_KEYS * N_KEYS
PEER_TOPK = 16
PEER_DKEY = 256
PEER_DHALF = PEER_DKEY // 2
PEER_PICKS = PEER_HEADS * PEER_TOPK
DEPTH = 1
ALPHA = (2.0 * DEPTH) ** 0.25
LN_EPS = 1e-5
NEG = -1e30
IN_WIDTH = Q_WIDTH + 2 * KV_WIDTH + 2 * C_WIDTH + 2 * D_MODEL

LANES = 128
SUBLANES = 8
MIB = 1024 * 1024

PEER_GROUP = SUBLANES
PEER_ROWS = PEER_GROUP * PEER_PICKS
PACK_CHUNKS = D_MODEL // (2 * LANES)
PEER_TOKENS_PER_STEP = 256
ROUTE_TILE = 512


def _layer_norm(x, g, b):
    mu = jnp.mean(x, axis=-1, keepdims=True)
    xc = x - mu
    var = jnp.mean(xc * xc, axis=-1, keepdims=True)
    return xc * lax.rsqrt(var + LN_EPS) * g + b


def _dot(a, b):
    return jnp.dot(a.astype(BF16), b.astype(BF16), preferred_element_type=F32)


def _dot_nt(a, b):
    return lax.dot_general(a.astype(BF16), b.astype(BF16), (((1,), (1,)), ((), ())),
                           preferred_element_type=F32)


def _inproj_kernel(x_ref, w_ref, g_ref, b_ref, q_ref, kv_ref, cu_ref, cvn_ref, gate_ref):
    z = jnp.dot(x_ref[...].astype(BF16), w_ref[...], preferred_element_type=F32)
    o = 0
    q_ref[...] = z[:, o:o + Q_WIDTH]
    o += Q_WIDTH
    kv_ref[...] = z[:, o:o + 2 * KV_WIDTH]
    o += 2 * KV_WIDTH
    cu_ref[...] = z[:, o:o + C_WIDTH]
    o += C_WIDTH
    cvn_ref[...] = _layer_norm(z[:, o:o + C_WIDTH], g_ref[...], b_ref[...])
    o += C_WIDTH
    gate_ref[...] = z[:, o:o + 2 * D_MODEL]


def _inproj(x, w_bf16, c_g, c_b, tm):
    T = x.shape[0]
    widths = (Q_WIDTH, 2 * KV_WIDTH, C_WIDTH, C_WIDTH, 2 * D_MODEL)
    row = lambda n: pl.BlockSpec((tm, n), lambda i: (i, 0))
    full = lambda a: pl.BlockSpec(a.shape, lambda i: (0,) * a.ndim)
    return pl.pallas_call(
        _inproj_kernel,
        grid=(T // tm,),
        in_specs=[row(D_MODEL), full(w_bf16), full(c_g), full(c_b)],
        out_specs=[row(n) for n in widths],
        out_shape=[jax.ShapeDtypeStruct((T, n), F32) for n in widths],
        compiler_params=pltpu.CompilerParams(
            dimension_semantics=("arbitrary",), vmem_limit_bytes=48 * MIB),
        name="inproj",
    )(x, w_bf16, c_g, c_b)


def _merge_and_norm(x, ya, yc, gate, wa_ref, wc_ref, wo_ref, g_ref, b_ref):
    ga = gate[:, :D_MODEL]
    gc = gate[:, D_MODEL:]
    mix = jax.nn.sigmoid(ga) * _dot(ya, wa_ref[...]) + jax.nn.sigmoid(gc) * _dot(yc, wc_ref[...])
    h = ALPHA * x + _dot(mix, wo_ref[...])
    return _layer_norm(h, g_ref[...], b_ref[...])


def _softmax_with_sink(s, sink):
    m = jnp.maximum(jnp.max(s, axis=-1, keepdims=True), sink)
    e = jnp.exp(s - m)
    return e / (jnp.sum(e, axis=-1, keepdims=True) + jnp.exp(sink - m))


PROMPT_BLOCKS_PER_STEP = 2


def _prompt_block_kernel(nb, sinks_ref, q_ref, kvp_ref, kvo_ref, cu_ref, cvn_ref, gate_ref,
                         x_ref, bias_ref, ws_ref, bs_ref, wa_ref, wc_ref, wo_ref, g_ref,
                         b_ref, x1_ref):
    blk = ATT_BLOCK
    qi = lax.broadcasted_iota(I32, (blk, 2 * blk), 0)
    kj = lax.broadcasted_iota(I32, (blk, 2 * blk), 1)
    dist = qi - kj + blk
    in_window = (dist >= 0) & (dist < WINDOW)
    ti = lax.broadcasted_iota(I32, (C_CHUNK, C_CHUNK), 0)
    si = lax.broadcasted_iota(I32, (C_CHUNK, C_CHUNK), 1)
    wms = [jnp.where(si <= ti, ws_ref[g], 0.0) for g in range(C_GROUPS)]

    scale = HEAD_DIM ** -0.5
    work = [(sb, h) for sb in range(PROMPT_BLOCKS_PER_STEP) for h in range(N_HEADS)]
    kvs, scores = {}, {}
    for sb in range(PROMPT_BLOCKS_PER_STEP):
        rows = slice(sb * blk, (sb + 1) * blk)
        kv_prev = kvp_ref[...] if sb == 0 else kvo_ref[(sb - 1) * blk:sb * blk, :]
        kvs[sb] = jnp.concatenate([kv_prev, kvo_ref[rows, :]], axis=0)
    for sb, h in work:
        n = lax.rem(pl.program_id(0) * PROMPT_BLOCKS_PER_STEP + sb, nb)
        mask = in_window & ((kj >= blk) | (n > 0))
        g = h // REP
        qh = q_ref[sb * blk:(sb + 1) * blk, h * HEAD_DIM:(h + 1) * HEAD_DIM]
        kg = kvs[sb][:, g * HEAD_DIM:(g + 1) * HEAD_DIM]
        scores[sb, h] = jnp.where(mask, _dot_nt(qh, kg) * scale + bias_ref[h], NEG)
    probs = {key: _softmax_with_sink(scores[key], sinks_ref[key[1]]) for key in work}
    ya_blocks, yc_blocks = [], []
    for sb in range(PROMPT_BLOCKS_PER_STEP):
        rows = slice(sb * blk, (sb + 1) * blk)
        outs = []
        for h in range(N_HEADS):
            g = h // REP
            vg = kvs[sb][:, KV_WIDTH + g * HEAD_DIM:KV_WIDTH + (g + 1) * HEAD_DIM]
            outs.append(_dot(probs[sb, h], vg))
        ya_blocks.append(jnp.concatenate(outs, axis=1))

        cvn = cvn_ref[rows, :]
        parts = [_dot(wms[g], cvn[:, g * C_GDIM:(g + 1) * C_GDIM]) for g in range(C_GROUPS)]
        yc_blocks.append(cu_ref[rows, :] * (jnp.concatenate(parts, axis=1) + bs_ref[...]))

    ya = jnp.concatenate(ya_blocks, axis=0)
    yc = jnp.concatenate(yc_blocks, axis=0)
    x1_ref[...] = _merge_and_norm(x_ref[...], ya, yc, gate_ref[...], wa_ref, wc_ref, wo_ref,
                                  g_ref, b_ref)


def _prompt_blocks(nb, sinks, q, kv, cu, cvn, gate, x, bias, ws, bs_exp,
                   wa, wc, wo, ln_g, ln_b):
    T = q.shape[0]
    blk = ATT_BLOCK
    tm = PROMPT_BLOCKS_PER_STEP * blk
    assert nb % PROMPT_BLOCKS_PER_STEP == 0
    row = lambda n: pl.BlockSpec((tm, n), lambda i: (i, 0))
    full = lambda a: pl.BlockSpec(a.shape, lambda i: (0,) * a.ndim)
    prev = pl.BlockSpec((blk, 2 * KV_WIDTH),
                        lambda i: (jnp.maximum(i * PROMPT_BLOCKS_PER_STEP - 1, 0), 0))
    return pl.pallas_call(
        functools.partial(_prompt_block_kernel, nb),
        grid=(T // tm,),
        in_specs=[pl.BlockSpec(memory_space=pltpu.SMEM),
                  row(Q_WIDTH), prev, row(2 * KV_WIDTH), row(C_WIDTH), row(C_WIDTH),
                  row(2 * D_MODEL), row(D_MODEL), full(bias), full(ws), full(bs_exp),
                  full(wa), full(wc), full(wo), full(ln_g), full(ln_b)],
        out_specs=row(D_MODEL),
        out_shape=jax.ShapeDtypeStruct((T, D_MODEL), F32),
        compiler_params=pltpu.CompilerParams(
            dimension_semantics=("arbitrary",), vmem_limit_bytes=48 * MIB),
        name="prompt_blocks",
    )(sinks, q, kv, kv, cu, cvn, gate, x, bias, ws, bs_exp, wa, wc, wo, ln_g, ln_b)


def _sample_block_kernel(seqs, s_new, sinks_ref, q_ref, kv_ref, cu_ref, cvn_ref, gate_ref,
                         x_ref, ck_ref, cv_ref, biasc_ref, biasn_ref, wbd_ref, bs_ref, wa_ref,
                         wc_ref, wo_ref, g_ref, b_ref, x1_ref, kout_ref, vout_ref):
    w_buf = ck_ref.shape[1]
    rows = REP * s_new
    qi_c = lax.rem(lax.broadcasted_iota(I32, (rows, w_buf), 0), s_new)
    kj_c = lax.broadcasted_iota(I32, (rows, w_buf), 1)
    mask_c = (qi_c + w_buf - kj_c) < WINDOW
    qi_n = lax.rem(lax.broadcasted_iota(I32, (rows, s_new), 0), s_new)
    kj_n = lax.broadcasted_iota(I32, (rows, s_new), 1)
    mask_n = kj_n <= qi_n
    sink_col = [jnp.concatenate([jnp.full((s_new, 1), sinks_ref[g * REP + r], F32)
                                 for r in range(REP)], axis=0) for g in range(N_KV)]

    q_all = q_ref[...]
    kv_all = kv_ref[...]
    scale = HEAD_DIM ** -0.5
    work = [(b, g) for b in range(seqs) for g in range(N_KV)]
    ck, cv, kvb, scores = {}, {}, {}, {}
    for b in range(seqs):
        kvb[b] = kv_all[b * s_new:(b + 1) * s_new]
        ck[b] = ck_ref[b]
        cv[b] = cv_ref[b]
        kout_ref[b] = jnp.concatenate([ck[b][s_new:], kvb[b][:, :KV_WIDTH]], axis=0)
        vout_ref[b] = jnp.concatenate([cv[b][s_new:], kvb[b][:, KV_WIDTH:]], axis=0)
    for b, g in work:
        qb = q_all[b * s_new:(b + 1) * s_new]
        qg = jnp.concatenate([qb[:, (g * REP + r) * HEAD_DIM:(g * REP + r + 1) * HEAD_DIM]
                              for r in range(REP)], axis=0)
        lane = slice(g * HEAD_DIM, (g + 1) * HEAD_DIM)
        sc = _dot_nt(qg, ck[b][:, lane]) * scale + biasc_ref[g]
        sn = _dot_nt(qg, kvb[b][:, lane]) * scale + biasn_ref[g]
        scores[b, g] = (jnp.where(mask_c, sc, NEG), jnp.where(mask_n, sn, NEG))
    probs = {}
    for b, g in work:
        sc, sn = scores[b, g]
        sink = sink_col[g]
        m = jnp.maximum(jnp.maximum(jnp.max(sc, axis=-1, keepdims=True),
                                    jnp.max(sn, axis=-1, keepdims=True)), sink)
        ec = jnp.exp(sc - m)
        en = jnp.exp(sn - m)
        den = (jnp.sum(ec, axis=-1, keepdims=True) + jnp.sum(en, axis=-1, keepdims=True)
               + jnp.exp(sink - m))
        probs[b, g] = (ec / den, en / den)
    ya_rows = []
    for b in range(seqs):
        heads = []
        for g in range(N_KV):
            lane = slice(g * HEAD_DIM, (g + 1) * HEAD_DIM)
            vlane = slice(KV_WIDTH + g * HEAD_DIM, KV_WIDTH + (g + 1) * HEAD_DIM)
            pc, pn = probs[b, g]
            og = _dot(pc, cv[b][:, lane]) + _dot(pn, kvb[b][:, vlane])
            heads.extend(og[r * s_new:(r + 1) * s_new] for r in range(REP))
        ya_rows.append(jnp.concatenate(heads, axis=1))
    ya = jnp.concatenate(ya_rows, axis=0)

    cvn = cvn_ref[...]
    parts = [_dot(wbd_ref[g], cvn[:, g * C_GDIM:(g + 1) * C_GDIM]) for g in range(C_GROUPS)]
    yc = cu_ref[...] * (jnp.concatenate(parts, axis=1) + bs_ref[...])

    x1_ref[...] = _merge_and_norm(x_ref[...], ya, yc, gate_ref[...], wa_ref, wc_ref, wo_ref,
                                  g_ref, b_ref)


def _sample_blocks(seqs, s_new, sinks, q, kv, cu, cvn, gate, x, cache_k,
                   cache_v, bias_c, bias_n, wbd, bs_exp, wa, wc, wo, ln_g, ln_b):
    T = q.shape[0]
    n_seq, w_buf, kvw = cache_k.shape
    tm = seqs * s_new
    row = lambda n: pl.BlockSpec((tm, n), lambda i: (i, 0))
    full = lambda a: pl.BlockSpec(a.shape, lambda i: (0,) * a.ndim)
    cache = pl.BlockSpec((seqs, w_buf, kvw), lambda i: (i, 0, 0))
    return pl.pallas_call(
        functools.partial(_sample_block_kernel, seqs, s_new),
        grid=(T // tm,),
        in_specs=[pl.BlockSpec(memory_space=pltpu.SMEM),
                  row(Q_WIDTH), row(2 * KV_WIDTH), row(C_WIDTH), row(C_WIDTH),
                  row(2 * D_MODEL), row(D_MODEL), cache, cache, full(bias_c), full(bias_n),
                  full(wbd), full(bs_exp), full(wa), full(wc), full(wo), full(ln_g),
                  full(ln_b)],
        out_specs=[row(D_MODEL), cache, cache],
        out_shape=[jax.ShapeDtypeStruct((T, D_MODEL), F32),
                   jax.ShapeDtypeStruct(cache_k.shape, F32),
                   jax.ShapeDtypeStruct(cache_v.shape, F32)],
        compiler_params=pltpu.CompilerParams(
            dimension_semantics=("arbitrary",), vmem_limit_bytes=48 * MIB),
        name="sample_blocks",
    )(sinks, q, kv, cu, cvn, gate, x, cache_k, cache_v, bias_c, bias_n, wbd, bs_exp, wa, wc,
      wo, ln_g, ln_b)


def _two_part_specs(tiles_a, tile_rows, width):
    first = pl.BlockSpec((tile_rows, width), lambda i: (jnp.minimum(i, tiles_a - 1), 0))
    second = pl.BlockSpec((tile_rows, width), lambda i: (jnp.maximum(i - tiles_a, 0), 0))
    return first, second


def _two_part_tile(tiles_a, a_ref, b_ref):
    return jnp.where(pl.program_id(0) < tiles_a, a_ref[...], b_ref[...])


def _top16(s, iota, fill):
    vals, idxs = [], []
    for _ in range(PEER_TOPK):
        m = jnp.max(s, axis=0, keepdims=True)
        i = jnp.min(jnp.where(s == m, iota, fill), axis=0, keepdims=True)
        vals.append(m)
        idxs.append(i)
        s = jnp.where(iota == i, -jnp.inf, s)
    return jnp.concatenate(vals, axis=0), jnp.concatenate(idxs, axis=0)


def _pick(table, iota, idx):
    return jnp.concatenate(
        [jnp.sum(jnp.where(iota == idx[r:r + 1], table, 0), axis=0, keepdims=True)
         for r in range(PEER_TOPK)], axis=0)


_CAND_SHORT = SUBLANES


def _cand_flat_index(tt):
    i16 = lax.broadcasted_iota(I32, (PEER_TOPK, tt), 0)
    i8 = lax.broadcasted_iota(I32, (_CAND_SHORT, tt), 0)
    pieces = [i16] + [k * PEER_TOPK + i8 for k in range(1, _CAND_SHORT)]
    pieces.append((_CAND_SHORT + i8) * PEER_TOPK)
    return jnp.concatenate(pieces, axis=0)


def _cand_values(sv0, sv1):
    pieces = [sv0[0:1] + sv1]
    pieces += [sv0[k:k + 1] + sv1[0:_CAND_SHORT] for k in range(1, _CAND_SHORT)]
    pieces.append(sv0[_CAND_SHORT:] + sv1[0:1])
    return jnp.concatenate(pieces, axis=0)


def _route_kernel(tiles_a, xa_ref, xb_ref, wq_ref, sk_ref, e_ref, g_ref, q_scr):
    tt = xa_ref.shape[0]
    x1 = _two_part_tile(tiles_a, xa_ref, xb_ref)
    qf = jnp.dot(x1.astype(BF16), wq_ref[...], preferred_element_type=F32)
    for hc in range(2 * PEER_HEADS):
        q_scr[hc] = qf[:, hc * PEER_DHALF:(hc + 1) * PEER_DHALF]
    iota_n = lax.broadcasted_iota(I32, (N_KEYS, tt), 0)
    iota_k = lax.broadcasted_iota(I32, (PEER_TOPK, tt), 0)
    flat = _cand_flat_index(tt)

    def first_stage(h):
        out = []
        for c in range(2):
            s = _dot_nt(sk_ref[h, c], q_scr[2 * h + c])
            out.extend(_top16(s, iota_n, N_KEYS))
        return out[0], out[2], out[1], out[3]

    def second_stage(h, carry):
        sv0, sv1, si0, si1 = carry
        fv, fi = _top16(_cand_values(sv0, sv1), flat, PEER_TOPK * PEER_TOPK)
        i1 = _pick(si0, iota_k, lax.shift_right_logical(fi, 4))
        i2 = _pick(si1, iota_k, lax.bitwise_and(fi, PEER_TOPK - 1))
        ex = jnp.exp(fv - fv[0:1])
        row0 = pl.multiple_of(h * PEER_TOPK, PEER_TOPK)
        e_ref[0, pl.ds(row0, PEER_TOPK), :] = (i1 * N_KEYS + i2) * PACK_CHUNKS
        g_ref[0, pl.ds(row0, PEER_TOPK), :] = ex / jnp.sum(ex, axis=0, keepdims=True)

    def body(h, carry):
        second_stage(h - 1, carry)
        return first_stage(h)

    last = lax.fori_loop(1, PEER_HEADS, body, first_stage(0))
    second_stage(PEER_HEADS - 1, last)


def _route(x1a, x1b, wq_bf16, subkeys, tt):
    T = x1a.shape[0] + x1b.shape[0]
    tiles_a = x1a.shape[0] // tt
    full = lambda a: pl.BlockSpec(a.shape, lambda i: (0,) * a.ndim)
    out = pl.BlockSpec((1, PEER_PICKS, tt), lambda i: (i, 0, 0))
    return pl.pallas_call(
        functools.partial(_route_kernel, tiles_a),
        grid=(T // tt,),
        in_specs=[*_two_part_specs(tiles_a, tt, D_MODEL), full(wq_bf16), full(subkeys)],
        out_specs=[out, out],
        out_shape=[jax.ShapeDtypeStruct((T // tt, PEER_PICKS, tt), I32),
                   jax.ShapeDtypeStruct((T // tt, PEER_PICKS, tt), F32)],
        scratch_shapes=[pltpu.VMEM((2 * PEER_HEADS, tt, PEER_DHALF), F32)],
        compiler_params=pltpu.CompilerParams(
            dimension_semantics=("arbitrary",), vmem_limit_bytes=48 * MIB),
        name="peer_route",
    )(x1a, x1b, wq_bf16, subkeys)


def _pack_kernel(t_ref, o_ref, rows_ref):
    tm = t_ref.shape[0]
    half = D_MODEL // 2
    for c in range(PACK_CHUNKS):
        rows_ref[pl.ds(2 * c, tm, stride=2 * PACK_CHUNKS), :] = t_ref[:, c * LANES:(c + 1) * LANES]
        rows_ref[pl.ds(2 * c + 1, tm, stride=2 * PACK_CHUNKS), :] = (
            t_ref[:, half + c * LANES:half + (c + 1) * LANES])
    o_ref[...] = pltpu.bitcast(rows_ref[...].astype(BF16), I32)


def _pack_table(tab, tm):
    n = tab.shape[0]
    return pl.pallas_call(
        _pack_kernel,
        grid=(n // tm,),
        in_specs=[pl.BlockSpec((tm, D_MODEL), lambda i: (i, 0))],
        out_specs=pl.BlockSpec((tm * PACK_CHUNKS, LANES), lambda i: (i, 0)),
        out_shape=jax.ShapeDtypeStruct((n * PACK_CHUNKS, LANES), I32),
        scratch_shapes=[pltpu.VMEM((2 * tm * PACK_CHUNKS, LANES), F32)],
        compiler_params=pltpu.CompilerParams(dimension_semantics=("arbitrary",)),
        name="pack_table",
    )(tab)


def _unpack(word):
    return pltpu.bitcast(word, BF16).astype(F32)


def _interleave_store(ref, even_rows, odd_rows):
    n = even_rows.shape[0]
    ref[pl.ds(0, n, stride=2), :] = even_rows
    ref[pl.ds(1, n, stride=2), :] = odd_rows


def _gather_group(idx_ref, tab_ref, tile_ref):
    for r in range(PEER_ROWS):
        row = pl.multiple_of(idx_ref[r], PACK_CHUNKS)
        tile_ref[pl.ds(r * PACK_CHUNKS, PACK_CHUNKS), :] = tab_ref[pl.ds(row, PACK_CHUNKS), :]


def _tile_rows(tile_ref, c, j):
    start = j * PEER_GROUP * PACK_CHUNKS + c
    return tile_ref[pl.ds(start, PEER_GROUP, stride=PACK_CHUNKS), :]


def _index_copy(e_hbm, group, buf, sem, slot):
    return pltpu.make_async_copy(e_hbm.at[group], buf, sem.at[slot])


def _for_each_group(e_hbm, idx_bufs, sem, groups_per_step, process):
    step = pl.program_id(0)
    total = pl.num_programs(0) * groups_per_step
    base = step * groups_per_step

    @pl.when(step == 0)
    def _():
        for slot in range(2):
            _index_copy(e_hbm, slot, idx_bufs[slot], sem, slot).start()

    def pair(p, carry):
        for slot in range(2):
            local = 2 * p + slot
            group = base + local
            _index_copy(e_hbm, group, idx_bufs[slot], sem, slot).wait()
            process(idx_bufs[slot], local)

            @pl.when(group + 2 < total)
            def _():
                _index_copy(e_hbm, group + 2, idx_bufs[slot], sem, slot).start()
        return carry

    lax.fori_loop(0, groups_per_step // 2, pair, 0)


def _peer_u_kernel(tiles_a, e_hbm, xa_ref, xb_ref, g_ref, tab_ref, w_ref, xi_ref, a2_ref,
                   tile_ref, idx_a, idx_b, sem):
    tb = g_ref.shape[0]
    rows = 2 * PEER_GROUP
    lane = lax.broadcasted_iota(I32, (rows, PEER_PICKS), 1)
    half = D_MODEL // 2
    x = _two_part_tile(tiles_a, xa_ref, xb_ref)
    for c in range(PACK_CHUNKS):
        _interleave_store(xi_ref.at[c], x[:, c * LANES:(c + 1) * LANES],
                          x[:, half + c * LANES:half + (c + 1) * LANES])

    def process(idx_ref, local):
        r0 = pl.multiple_of(local * rows, rows)
        _gather_group(idx_ref, tab_ref, tile_ref)
        xi = [xi_ref[c, pl.ds(r0, rows), :] for c in range(PACK_CHUNKS)]
        a2 = jnp.zeros((rows, PEER_PICKS), F32)
        for j in range(PEER_PICKS):
            p = None
            for c in range(PACK_CHUNKS):
                term = _unpack(_tile_rows(tile_ref, c, j)) * xi[c]
                p = term if p is None else p + term
            a2 = jnp.where(lane == j, jnp.sum(p, axis=1, keepdims=True), a2)
        a2_ref[pl.ds(r0, rows), :] = a2

    _for_each_group(e_hbm, (idx_a, idx_b), sem, tb // PEER_GROUP, process)
    a = a2_ref[pl.ds(0, tb, stride=2), :] + a2_ref[pl.ds(1, tb, stride=2), :]
    w_ref[...] = g_ref[...] * jax.nn.gelu(a)


def _peer_v_kernel(e_hbm, w_ref, tab_ref, f_ref, w2_ref, acc_ref, tile_ref, idx_a, idx_b, sem):
    tb = w_ref.shape[0]
    rows = 2 * PEER_GROUP
    half = D_MODEL // 2
    _interleave_store(w2_ref, w_ref[...], w_ref[...])

    def process(idx_ref, local):
        r0 = pl.multiple_of(local * rows, rows)
        _gather_group(idx_ref, tab_ref, tile_ref)
        w2 = w2_ref[pl.ds(r0, rows), :]
        acc = [jnp.zeros((rows, LANES), F32) for _ in range(PACK_CHUNKS)]
        for j in range(PEER_PICKS):
            wb = jnp.broadcast_to(w2[:, j:j + 1], (rows, LANES))
            for c in range(PACK_CHUNKS):
                acc[c] = acc[c] + wb * _unpack(_tile_rows(tile_ref, c, j))
        for c in range(PACK_CHUNKS):
            acc_ref[c, pl.ds(r0, rows), :] = acc[c]

    _for_each_group(e_hbm, (idx_a, idx_b), sem, tb // PEER_GROUP, process)
    for c in range(PACK_CHUNKS):
        f_ref[:, c * LANES:(c + 1) * LANES] = acc_ref[c, pl.ds(0, tb, stride=2), :]
        f_ref[:, half + c * LANES:half + (c + 1) * LANES] = acc_ref[c, pl.ds(1, tb, stride=2), :]


def _peer_specs(tb):
    assert (tb // PEER_GROUP) % 2 == 0
    idx = pl.BlockSpec(memory_space=pl.ANY)
    picks = pl.BlockSpec((tb, PEER_PICKS), lambda i: (i, 0))
    feat = pl.BlockSpec((tb, D_MODEL), lambda i: (i, 0))
    table = pl.BlockSpec((N_EXPERTS * PACK_CHUNKS, LANES), lambda i: (0, 0),
                         pipeline_mode=pl.Buffered(1))
    scratch = [pltpu.VMEM((PACK_CHUNKS * PEER_ROWS, LANES), I32),
               pltpu.SMEM((PEER_ROWS,), I32), pltpu.SMEM((PEER_ROWS,), I32),
               pltpu.SemaphoreType.DMA((2,))]
    params = pltpu.CompilerParams(dimension_semantics=("arbitrary",),
                                  vmem_limit_bytes=48 * MIB)
    return idx, picks, feat, table, scratch, params


def _peer_u(e_grp, x1a, x1b, g, tab_u, tb):
    T = x1a.shape[0] + x1b.shape[0]
    tiles_a = x1a.shape[0] // tb
    idx, picks, feat, table, scratch, params = _peer_specs(tb)
    return pl.pallas_call(
        functools.partial(_peer_u_kernel, tiles_a), grid=(T // tb,),
        in_specs=[idx, *_two_part_specs(tiles_a, tb, D_MODEL), picks, table], out_specs=picks,
        out_shape=jax.ShapeDtypeStruct((T, PEER_PICKS), F32),
        scratch_shapes=[pltpu.VMEM((PACK_CHUNKS, 2 * tb, LANES), F32),
                        pltpu.VMEM((2 * tb, PEER_PICKS), F32)] + scratch,
        compiler_params=params,
        name="peer_u",
    )(e_grp, x1a, x1b, g, tab_u)


def _peer_v(e_grp, w, tab_v, tb):
    T = w.shape[0]
    idx, picks, feat, table, scratch, params = _peer_specs(tb)
    return pl.pallas_call(
        _peer_v_kernel, grid=(T // tb,),
        in_specs=[idx, picks, table], out_specs=feat,
        out_shape=jax.ShapeDtypeStruct((T, D_MODEL), F32),
        scratch_shapes=[pltpu.VMEM((2 * tb, PEER_PICKS), F32),
                        pltpu.VMEM((PACK_CHUNKS, 2 * tb, LANES), F32)] + scratch,
        compiler_params=params, name="peer_v",
    )(e_grp, w, tab_v)


def _final_kernel(x1_ref, f_ref, p_ref, wg_ref, wp_ref, g_ref, b_ref, y_ref):
    x1 = x1_ref[...]
    e = jax.nn.sigmoid(_dot(x1, wg_ref[...])) * _dot(p_ref[...], wp_ref[...])
    y_ref[...] = _layer_norm(ALPHA * x1 + f_ref[...] + e, g_ref[...], b_ref[...])


def _final(x1, f, f_row_offset, p, wg, wp, ln_g, ln_b, tm):
    T = p.shape[0]
    off = f_row_offset // tm
    full = lambda a: pl.BlockSpec(a.shape, lambda i: (0,) * a.ndim)
    shifted = pl.BlockSpec((tm, D_MODEL), lambda i: (i + off, 0))
    return pl.pallas_call(
        _final_kernel,
        grid=(T // tm,),
        in_specs=[pl.BlockSpec((tm, D_MODEL), lambda i: (i, 0)), shifted,
                  pl.BlockSpec((tm, PLE_DIM), lambda i: (i, 0)),
                  full(wg), full(wp), full(ln_g), full(ln_b)],
        out_specs=pl.BlockSpec((tm, D_MODEL), lambda i: (i, 0)),
        out_shape=jax.ShapeDtypeStruct((T, D_MODEL), F32),
        compiler_params=pltpu.CompilerParams(
            dimension_semantics=("arbitrary",), vmem_limit_bytes=48 * MIB),
        name="final",
    )(x1, f, p, wg, wp, ln_g, ln_b)


def _t5_bias_by_distance(dist, table):
    n = jnp.maximum(dist, 0)
    max_exact = N_BUCKETS // 2
    nf = jnp.maximum(n, 1).astype(F32)
    large = max_exact + jnp.floor(jnp.log(nf / max_exact) / math.log(MAX_DISTANCE / max_exact)
                                  * (N_BUCKETS - max_exact)).astype(I32)
    large = jnp.minimum(large, N_BUCKETS - 1)
    bucket = jnp.where(n < max_exact, n, large)
    hit = bucket[None, :, None] == jnp.arange(N_BUCKETS)
    return jnp.sum(jnp.where(hit, table.T[:, None, :], 0.0), axis=-1)


def _band_bias(table, n_query, n_key, key_offset):
    d_max = n_query - 1 + key_offset
    d_min = key_offset - (n_key - 1)
    by_dist = _t5_bias_by_distance(jnp.arange(d_max, d_min - 1, -1), table)
    rows = [by_dist[:, d_max - (q + key_offset):d_max - (q + key_offset) + n_key]
            for q in range(n_query)]
    return jnp.stack(rows, axis=1)


def kernel(x_prompt, x_sample, cache_k_win, cache_v_win, p_prompt, p_sample, rel_bias_table,
           w_in, attn_sinks, w_att_out, c_ln_g, c_ln_b, c_ws, c_bs, w_chunk_out, w_o, ln1_g,
           ln1_b, peer_wq, peer_subkeys, peer_u, peer_v, w_ple_gate, w_ple_proj, ln2_g, ln2_b):
    batch, seq, d = x_prompt.shape
    n_seq, s_new, _ = x_sample.shape
    w_buf = cache_k_win.shape[2]
    assert d == D_MODEL and w_in.shape[0] == DEPTH and seq % ATT_BLOCK == 0
    tp = batch * seq
    ts = n_seq * s_new
    t_all = tp + ts
    nb = seq // ATT_BLOCK
    seqs_per_step = 8
    assert n_seq % seqs_per_step == 0 and tp % (seqs_per_step * s_new) == 0
    assert tp % 256 == 0 and ts % 256 == 0 and tp % PEER_TOKENS_PER_STEP == 0
    assert tp % ROUTE_TILE == 0 and ts % ROUTE_TILE == 0

    row2 = lambda v: v.reshape(1, -1)
    w_in_b = w_in[0].astype(BF16)
    wa, wc, wo = w_att_out[0].astype(BF16), w_chunk_out[0].astype(BF16), w_o[0].astype(BF16)
    sinks = attn_sinks[0]
    cg, cb = row2(c_ln_g[0]), row2(c_ln_b[0])
    l1g, l1b = row2(ln1_g[0]), row2(ln1_b[0])
    l2g, l2b = row2(ln2_g[0]), row2(ln2_b[0])
    ws, bs = c_ws[0], c_bs[0]

    bias_p = _band_bias(rel_bias_table, ATT_BLOCK, 2 * ATT_BLOCK, ATT_BLOCK)
    bias_s = _band_bias(rel_bias_table, s_new, w_buf + s_new, w_buf)
    bias_s = bias_s.reshape(N_KV, REP * s_new, w_buf + s_new)
    bias_sc, bias_sn = bias_s[:, :, :w_buf], bias_s[:, :, w_buf:]

    bs_exp_p = jnp.repeat(bs.T, C_GDIM, axis=1)
    bs_exp_s = jnp.tile(jnp.repeat(bs[:, :s_new].T, C_GDIM, axis=1), (seqs_per_step, 1))
    tril = jnp.tril(jnp.ones((s_new, s_new), F32))
    eye = jnp.eye(seqs_per_step, dtype=F32)
    wbd = jnp.stack([jnp.kron(eye, ws[g, :s_new, :s_new] * tril) for g in range(C_GROUPS)])

    xp = x_prompt.reshape(tp, d)
    q, kv, cu, cvn, gate = _inproj(xp, w_in_b, cg, cb, 256)
    x1p = _prompt_blocks(nb, sinks, q, kv, cu, cvn, gate, xp, bias_p, ws, bs_exp_p,
                        wa, wc, wo, l1g, l1b)
    kv_tail = kv.reshape(batch, seq, 2 * KV_WIDTH)[:, seq - w_buf:]
    kp = kv_tail[..., :KV_WIDTH].reshape(1, batch, w_buf, N_KV, HEAD_DIM)
    vp = kv_tail[..., KV_WIDTH:].reshape(1, batch, w_buf, N_KV, HEAD_DIM)

    xs = x_sample.reshape(ts, d)
    q, kv, cu, cvn_s, gate = _inproj(xs, w_in_b, cg, cb, 256)
    x1s, k_new, v_new = _sample_blocks(
        seqs_per_step, s_new, sinks, q, kv, cu, cvn_s, gate, xs,
        cache_k_win[0].reshape(n_seq, w_buf, KV_WIDTH),
        cache_v_win[0].reshape(n_seq, w_buf, KV_WIDTH),
        bias_sc, bias_sn, wbd, bs_exp_s, wa, wc, wo, l1g, l1b)
    ks_out = k_new.reshape(1, n_seq, w_buf, N_KV, HEAD_DIM)
    vs_out = v_new.reshape(1, n_seq, w_buf, N_KV, HEAD_DIM)
    cs_out = cvn_s.reshape(1, n_seq, s_new, C_WIDTH)

    e_t, g_t = _route(x1p, x1s, peer_wq[0].astype(BF16), peer_subkeys[0], ROUTE_TILE)
    nt = t_all // ROUTE_TILE
    e_grp = (e_t.reshape(nt, PEER_PICKS, ROUTE_TILE // PEER_GROUP, PEER_GROUP)
             .transpose(0, 2, 1, 3).reshape(t_all // PEER_GROUP, PEER_ROWS))
    g_tok = g_t.transpose(0, 2, 1).reshape(t_all, PEER_PICKS)
    tab_u = _pack_table(peer_u[0], 256)
    tab_v = _pack_table(peer_v[0], 256)
    w_tok = _peer_u(e_grp, x1p, x1s, g_tok, tab_u, PEER_TOKENS_PER_STEP)
    f = _peer_v(e_grp, w_tok, tab_v, PEER_TOKENS_PER_STEP)

    wg, wp = w_ple_gate[0].astype(BF16), w_ple_proj[0].astype(BF16)
    yp = _final(x1p, f, 0, p_prompt[0].reshape(tp, PLE_DIM), wg, wp, l2g, l2b,
                512 if tp % 512 == 0 else 256)
    ys = _final(x1s, f, tp, p_sample[0].reshape(ts, PLE_DIM), wg, wp, l2g, l2b, 256)
    return (yp.reshape(batch, seq, d), ys.reshape(n_seq, s_new, d), kp, vp, ks_out, vs_out,
            cs_out)
```

```python
import functools
import math

import jax
import jax.numpy as jnp
from jax import lax
from jax.experimental import pallas as pl
from jax.experimental.pallas import tpu as pltpu

F32 = jnp.float32
BF16 = jnp.bfloat16
I32 = jnp.int32

D_MODEL = 1024
N_HEADS = 8
N_KV = 2
REP = N_HEADS // N_KV
HEAD_DIM = 64
Q_WIDTH = N_HEADS * HEAD_DIM
KV_WIDTH = N_KV * HEAD_DIM
WINDOW = 128
ATT_BLOCK = 128
N_BUCKETS = 32
MAX_DISTANCE = 128
C_GROUPS = 4
C_CHUNK = 128
C_WIDTH = 512
C_GDIM = C_WIDTH // C_GROUPS
PLE_DIM = 256
PEER_HEADS = 8
N_KEYS = 128
N_EXPERTS = N_KEYS * N_KEYS
PEER_TOPK = 16
PEER_DKEY = 256
PEER_DHALF = PEER_DKEY // 2
PEER_PICKS = PEER_HEADS * PEER_TOPK
DEPTH = 1
ALPHA = (2.0 * DEPTH) ** 0.25
LN_EPS = 1e-5
NEG = -1e30
IN_WIDTH = Q_WIDTH + 2 * KV_WIDTH + 2 * C_WIDTH + 2 * D_MODEL

LANES = 128
SUBLANES = 8
MIB = 1024 * 1024

PEER_GROUP = SUBLANES
PEER_ROWS = PEER_GROUP * PEER_PICKS
PACK_CHUNKS = D_MODEL // (2 * LANES)
PEER_TOKENS_PER_STEP = 256
ROUTE_TILE = 512


def _layer_norm(x, g, b):
    mu = jnp.mean(x, axis=-1, keepdims=True)
    xc = x - mu
    var = jnp.mean(xc * xc, axis=-1, keepdims=True)
    return xc * lax.rsqrt(var + LN_EPS) * g + b


def _dot(a, b):
    return jnp.dot(a.astype(BF16), b.astype(BF16), preferred_element_type=F32)


def _dot_nt(a, b):
    return lax.dot_general(a.astype(BF16), b.astype(BF16), (((1,), (1,)), ((), ())),
                           preferred_element_type=F32)


def _inproj_kernel(x_ref, w_ref, g_ref, b_ref, q_ref, kv_ref, cu_ref, cvn_ref, gate_ref):
    z = jnp.dot(x_ref[...].astype(BF16), w_ref[...], preferred_element_type=F32)
    o = 0
    q_ref[...] = z[:, o:o + Q_WIDTH]
    o += Q_WIDTH
    kv_ref[...] = z[:, o:o + 2 * KV_WIDTH]
    o += 2 * KV_WIDTH
    cu_ref[...] = z[:, o:o + C_WIDTH]
    o += C_WIDTH
    cvn_ref[...] = _layer_norm(z[:, o:o + C_WIDTH], g_ref[...], b_ref[...])
    o += C_WIDTH
    gate_ref[...] = z[:, o:o + 2 * D_MODEL]


def _inproj(x, w_bf16, c_g, c_b, tm):
    T = x.shape[0]
    widths = (Q_WIDTH, 2 * KV_WIDTH, C_WIDTH, C_WIDTH, 2 * D_MODEL)
    row = lambda n: pl.BlockSpec((tm, n), lambda i: (i, 0))
    full = lambda a: pl.BlockSpec(a.shape, lambda i: (0,) * a.ndim)
    return pl.pallas_call(
        _inproj_kernel,
        grid=(T // tm,),
        in_specs=[row(D_MODEL), full(w_bf16), full(c_g), full(c_b)],
        out_specs=[row(n) for n in widths],
        out_shape=[jax.ShapeDtypeStruct((T, n), F32) for n in widths],
        compiler_params=pltpu.CompilerParams(
            dimension_semantics=("arbitrary",), vmem_limit_bytes=48 * MIB),
        name="inproj",
    )(x, w_bf16, c_g, c_b)


def _merge_and_norm(x, ya, yc, gate, wa_ref, wc_ref, wo_ref, g_ref, b_ref):
    ga = gate[:, :D_MODEL]
    gc = gate[:, D_MODEL:]
    mix = jax.nn.sigmoid(ga) * _dot(ya, wa_ref[...]) + jax.nn.sigmoid(gc) * _dot(yc, wc_ref[...])
    h = ALPHA * x + _dot(mix, wo_ref[...])
    return _layer_norm(h, g_ref[...], b_ref[...])


def _softmax_with_sink(s, sink):
    m = jnp.maximum(jnp.max(s, axis=-1, keepdims=True), sink)
    e = jnp.exp(s - m)
    return e / (jnp.sum(e, axis=-1, keepdims=True) + jnp.exp(sink - m))


PROMPT_BLOCKS_PER_STEP = 2


def _prompt_block_kernel(nb, sinks_ref, q_ref, kvp_ref, kvo_ref, cu_ref, cvn_ref, gate_ref,
                         x_ref, bias_ref, ws_ref, bs_ref, wa_ref, wc_ref, wo_ref, g_ref,
                         b_ref, x1_ref):
    blk = ATT_BLOCK
    qi = lax.broadcasted_iota(I32, (blk, 2 * blk), 0)
    kj = lax.broadcasted_iota(I32, (blk, 2 * blk), 1)
    dist = qi - kj + blk
    in_window = (dist >= 0) & (dist < WINDOW)
    ti = lax.broadcasted_iota(I32, (C_CHUNK, C_CHUNK), 0)
    si = lax.broadcasted_iota(I32, (C_CHUNK, C_CHUNK), 1)
    wms = [jnp.where(si <= ti, ws_ref[g], 0.0) for g in range(C_GROUPS)]

    scale = HEAD_DIM ** -0.5
    work = [(sb, h) for sb in range(PROMPT_BLOCKS_PER_STEP) for h in range(N_HEADS)]
    kvs, scores = {}, {}
    for sb in range(PROMPT_BLOCKS_PER_STEP):
        rows = slice(sb * blk, (sb + 1) * blk)
        kv_prev = kvp_ref[...] if sb == 0 else kvo_ref[(sb - 1) * blk:sb * blk, :]
        kvs[sb] = jnp.concatenate([kv_prev, kvo_ref[rows, :]], axis=0)
    for sb, h in work:
        n = lax.rem(pl.program_id(0) * PROMPT_BLOCKS_PER_STEP + sb, nb)
        mask = in_window & ((kj >= blk) | (n > 0))
        g = h // REP
        qh = q_ref[sb * blk:(sb + 1) * blk, h * HEAD_DIM:(h + 1) * HEAD_DIM]
        kg = kvs[sb][:, g * HEAD_DIM:(g + 1) * HEAD_DIM]
        scores[sb, h] = jnp.where(mask, _dot_nt(qh, kg) * scale + bias_ref[h], NEG)
    probs = {key: _softmax_with_sink(scores[key], sinks_ref[key[1]]) for key in work}
    ya_blocks, yc_blocks = [], []
    for sb in range(PROMPT_BLOCKS_PER_STEP):
        rows = slice(sb * blk, (sb + 1) * blk)
        outs = []
        for h in range(N_HEADS):
            g = h // REP
            vg = kvs[sb][:, KV_WIDTH + g * HEAD_DIM:KV_WIDTH + (g + 1) * HEAD_DIM]
            outs.append(_dot(probs[sb, h], vg))
        ya_blocks.append(jnp.concatenate(outs, axis=1))

        cvn = cvn_ref[rows, :]
        parts = [_dot(wms[g], cvn[:, g * C_GDIM:(g + 1) * C_GDIM]) for g in range(C_GROUPS)]
        yc_blocks.append(cu_ref[rows, :] * (jnp.concatenate(parts, axis=1) + bs_ref[...]))

    ya = jnp.concatenate(ya_blocks, axis=0)
    yc = jnp.concatenate(yc_blocks, axis=0)
    x1_ref[...] = _merge_and_norm(x_ref[...], ya, yc, gate_ref[...], wa_ref, wc_ref, wo_ref,
                                  g_ref, b_ref)


def _prompt_blocks(nb, sinks, q, kv, cu, cvn, gate, x, bias, ws, bs_exp,
                   wa, wc, wo, ln_g, ln_b):
    T = q.shape[0]
    blk = ATT_BLOCK
    tm = PROMPT_BLOCKS_PER_STEP * blk
    assert nb % PROMPT_BLOCKS_PER_STEP == 0
    row = lambda n: pl.BlockSpec((tm, n), lambda i: (i, 0))
    full = lambda a: pl.BlockSpec(a.shape, lambda i: (0,) * a.ndim)
    prev = pl.BlockSpec((blk, 2 * KV_WIDTH),
                        lambda i: (jnp.maximum(i * PROMPT_BLOCKS_PER_STEP - 1, 0), 0))
    return pl.pallas_call(
        functools.partial(_prompt_block_kernel, nb),
        grid=(T // tm,),
        in_specs=[pl.BlockSpec(memory_space=pltpu.SMEM),
                  row(Q_WIDTH), prev, row(2 * KV_WIDTH), row(C_WIDTH), row(C_WIDTH),
                  row(2 * D_MODEL), row(D_MODEL), full(bias), full(ws), full(bs_exp),
                  full(wa), full(wc), full(wo), full(ln_g), full(ln_b)],
        out_specs=row(D_MODEL),
        out_shape=jax.ShapeDtypeStruct((T, D_MODEL), F32),
        compiler_params=pltpu.CompilerParams(
            dimension_semantics=("arbitrary",), vmem_limit_bytes=48 * MIB),
        name="prompt_blocks",
    )(sinks, q, kv, kv, cu, cvn, gate, x, bias, ws, bs_exp, wa, wc, wo, ln_g, ln_b)


def _sample_block_kernel(seqs, s_new, sinks_ref, q_ref, kv_ref, cu_ref, cvn_ref, gate_ref,
                         x_ref, ck_ref, cv_ref, biasc_ref, biasn_ref, wbd_ref, bs_ref, wa_ref,
                         wc_ref, wo_ref, g_ref, b_ref, x1_ref, kout_ref, vout_ref):
    w_buf = ck_ref.shape[1]
    rows = REP * s_new
    qi_c = lax.rem(lax.broadcasted_iota(I32, (rows, w_buf), 0), s_new)
    kj_c = lax.broadcasted_iota(I32, (rows, w_buf), 1)
    mask_c = (qi_c + w_buf - kj_c) < WINDOW
    qi_n = lax.rem(lax.broadcasted_iota(I32, (rows, s_new), 0), s_new)
    kj_n = lax.broadcasted_iota(I32, (rows, s_new), 1)
    mask_n = kj_n <= qi_n
    sink_col = [jnp.concatenate([jnp.full((s_new, 1), sinks_ref[g * REP + r], F32)
                                 for r in range(REP)], axis=0) for g in range(N_KV)]

    q_all = q_ref[...]
    kv_all = kv_ref[...]
    scale = HEAD_DIM ** -0.5
    work = [(b, g) for b in range(seqs) for g in range(N_KV)]
    ck, cv, kvb, scores = {}, {}, {}, {}
    for b in range(seqs):
        kvb[b] = kv_all[b * s_new:(b + 1) * s_new]
        ck[b] = ck_ref[b]
        cv[b] = cv_ref[b]
        kout_ref[b] = jnp.concatenate([ck[b][s_new:], kvb[b][:, :KV_WIDTH]], axis=0)
        vout_ref[b] = jnp.concatenate([cv[b][s_new:], kvb[b][:, KV_WIDTH:]], axis=0)
    for b, g in work:
        qb = q_all[b * s_new:(b + 1) * s_new]
        qg = jnp.concatenate([qb[:, (g * REP + r) * HEAD_DIM:(g * REP + r + 1) * HEAD_DIM]
                              for r in range(REP)], axis=0)
        lane = slice(g * HEAD_DIM, (g + 1) * HEAD_DIM)
        sc = _dot_nt(qg, ck[b][:, lane]) * scale + biasc_ref[g]
        sn = _dot_nt(qg, kvb[b][:, lane]) * scale + biasn_ref[g]
        scores[b, g] = (jnp.where(mask_c, sc, NEG), jnp.where(mask_n, sn, NEG))
    probs = {}
    for b, g in work:
        sc, sn = scores[b, g]
        sink = sink_col[g]
        m = jnp.maximum(jnp.maximum(jnp.max(sc, axis=-1, keepdims=True),
                                    jnp.max(sn, axis=-1, keepdims=True)), sink)
        ec = jnp.exp(sc - m)
        en = jnp.exp(sn - m)
        den = (jnp.sum(ec, axis=-1, keepdims=True) + jnp.sum(en, axis=-1, keepdims=True)
               + jnp.exp(sink - m))
        probs[b, g] = (ec / den, en / den)
    ya_rows = []
    for b in range(seqs):
        heads = []
        for g in range(N_KV):
            lane = slice(g * HEAD_DIM, (g + 1) * HEAD_DIM)
            vlane = slice(KV_WIDTH + g * HEAD_DIM, KV_WIDTH + (g + 1) * HEAD_DIM)
            pc, pn = probs[b, g]
            og = _dot(pc, cv[b][:, lane]) + _dot(pn, kvb[b][:, vlane])
            heads.extend(og[r * s_new:(r + 1) * s_new] for r in range(REP))
        ya_rows.append(jnp.concatenate(heads, axis=1))
    ya = jnp.concatenate(ya_rows, axis=0)

    cvn = cvn_ref[...]
    parts = [_dot(wbd_ref[g], cvn[:, g * C_GDIM:(g + 1) * C_GDIM]) for g in range(C_GROUPS)]
    yc = cu_ref[...] * (jnp.concatenate(parts, axis=1) + bs_ref[...])

    x1_ref[...] = _merge_and_norm(x_ref[...], ya, yc, gate_ref[...], wa_ref, wc_ref, wo_ref,
                                  g_ref, b_ref)


def _sample_blocks(seqs, s_new, sinks, q, kv, cu, cvn, gate, x, cache_k,
                   cache_v, bias_c, bias_n, wbd, bs_exp, wa, wc, wo, ln_g, ln_b):
    T = q.shape[0]
    n_seq, w_buf, kvw = cache_k.shape
    tm = seqs * s_new
    row = lambda n: pl.BlockSpec((tm, n), lambda i: (i, 0))
    full = lambda a: pl.BlockSpec(a.shape, lambda i: (0,) * a.ndim)
    cache = pl.BlockSpec((seqs, w_buf, kvw), lambda i: (i, 0, 0))
    return pl.pallas_call(
        functools.partial(_sample_block_kernel, seqs, s_new),
        grid=(T // tm,),
        in_specs=[pl.BlockSpec(memory_space=pltpu.SMEM),
                  row(Q_WIDTH), row(2 * KV_WIDTH), row(C_WIDTH), row(C_WIDTH),
                  row(2 * D_MODEL), row(D_MODEL), cache, cache, full(bias_c), full(bias_n),
                  full(wbd), full(bs_exp), full(wa), full(wc), full(wo), full(ln_g),
                  full(ln_b)],
        out_specs=[row(D_MODEL), cache, cache],
        out_shape=[jax.ShapeDtypeStruct((T, D_MODEL), F32),
                   jax.ShapeDtypeStruct(cache_k.shape, F32),
                   jax.ShapeDtypeStruct(cache_v.shape, F32)],
        compiler_params=pltpu.CompilerParams(
            dimension_semantics=("arbitrary",), vmem_limit_bytes=48 * MIB),
        name="sample_blocks",
    )(sinks, q, kv, cu, cvn, gate, x, cache_k, cache_v, bias_c, bias_n, wbd, bs_exp, wa, wc,
      wo, ln_g, ln_b)


def _two_part_specs(tiles_a, tile_rows, width):
    first = pl.BlockSpec((tile_rows, width), lambda i: (jnp.minimum(i, tiles_a - 1), 0))
    second = pl.BlockSpec((tile_rows, width), lambda i: (jnp.maximum(i - tiles_a, 0), 0))
    return first, second


def _two_part_tile(tiles_a, a_ref, b_ref):
    return jnp.where(pl.program_id(0) < tiles_a, a_ref[...], b_ref[...])


def _top16(s, iota, fill):
    vals, idxs = [], []
    for _ in range(PEER_TOPK):
        m = jnp.max(s, axis=0, keepdims=True)
        i = jnp.min(jnp.where(s == m, iota, fill), axis=0, keepdims=True)
        vals.append(m)
        idxs.append(i)
        s = jnp.where(iota == i, -jnp.inf, s)
    return jnp.concatenate(vals, axis=0), jnp.concatenate(idxs, axis=0)


def _pick(table, iota, idx):
    return jnp.concatenate(
        [jnp.sum(jnp.where(iota == idx[r:r + 1], table, 0), axis=0, keepdims=True)
         for r in range(PEER_TOPK)], axis=0)


_CAND_SHORT = SUBLANES


def _cand_flat_index(tt):
    i16 = lax.broadcasted_iota(I32, (PEER_TOPK, tt), 0)
    i8 = lax.broadcasted_iota(I32, (_CAND_SHORT, tt), 0)
    pieces = [i16] + [k * PEER_TOPK + i8 for k in range(1, _CAND_SHORT)]
    pieces.append((_CAND_SHORT + i8) * PEER_TOPK)
    return jnp.concatenate(pieces, axis=0)


def _cand_values(sv0, sv1):
    pieces = [sv0[0:1] + sv1]
    pieces += [sv0[k:k + 1] + sv1[0:_CAND_SHORT] for k in range(1, _CAND_SHORT)]
    pieces.append(sv0[_CAND_SHORT:] + sv1[0:1])
    return jnp.concatenate(pieces, axis=0)


def _route_kernel(tiles_a, xa_ref, xb_ref, wq_ref, sk_ref, e_ref, g_ref, q_scr):
    tt = xa_ref.shape[0]
    x1 = _two_part_tile(tiles_a, xa_ref, xb_ref)
    qf = jnp.dot(x1.astype(BF16), wq_ref[...], preferred_element_type=F32)
    for hc in range(2 * PEER_HEADS):
        q_scr[hc] = qf[:, hc * PEER_DHALF:(hc + 1) * PEER_DHALF]
    iota_n = lax.broadcasted_iota(I32, (N_KEYS, tt), 0)
    iota_k = lax.broadcasted_iota(I32, (PEER_TOPK, tt), 0)
    flat = _cand_flat_index(tt)

    def first_stage(h):
        out = []
        for c in range(2):
            s = _dot_nt(sk_ref[h, c], q_scr[2 * h + c])
            out.extend(_top16(s, iota_n, N_KEYS))
        return out[0], out[2], out[1], out[3]

    def second_stage(h, carry):
        sv0, sv1, si0, si1 = carry
        fv, fi = _top16(_cand_values(sv0, sv1), flat, PEER_TOPK * PEER_TOPK)
        i1 = _pick(si0, iota_k, lax.shift_right_logical(fi, 4))
        i2 = _pick(si1, iota_k, lax.bitwise_and(fi, PEER_TOPK - 1))
        ex = jnp.exp(fv - fv[0:1])
        row0 = pl.multiple_of(h * PEER_TOPK, PEER_TOPK)
        e_ref[0, pl.ds(row0, PEER_TOPK), :] = (i1 * N_KEYS + i2) * PACK_CHUNKS
        g_ref[0, pl.ds(row0, PEER_TOPK), :] = ex / jnp.sum(ex, axis=0, keepdims=True)

    def body(h, carry):
        second_stage(h - 1, carry)
        return first_stage(h)

    last = lax.fori_loop(1, PEER_HEADS, body, first_stage(0))
    second_stage(PEER_HEADS - 1, last)


def _route(x1a, x1b, wq_bf16, subkeys, tt):
    T = x1a.shape[0] + x1b.shape[0]
    tiles_a = x1a.shape[0] // tt
    full = lambda a: pl.BlockSpec(a.shape, lambda i: (0,) * a.ndim)
    out = pl.BlockSpec((1, PEER_PICKS, tt), lambda i: (i, 0, 0))
    return pl.pallas_call(
        functools.partial(_route_kernel, tiles_a),
        grid=(T // tt,),
        in_specs=[*_two_part_specs(tiles_a, tt, D_MODEL), full(wq_bf16), full(subkeys)],
        out_specs=[out, out],
        out_shape=[jax.ShapeDtypeStruct((T // tt, PEER_PICKS, tt), I32),
                   jax.ShapeDtypeStruct((T // tt, PEER_PICKS, tt), F32)],
        scratch_shapes=[pltpu.VMEM((2 * PEER_HEADS, tt, PEER_DHALF), F32)],
        compiler_params=pltpu.CompilerParams(
            dimension_semantics=("arbitrary",), vmem_limit_bytes=48 * MIB),
        name="peer_route",
    )(x1a, x1b, wq_bf16, subkeys)


def _pack_kernel(t_ref, o_ref):
    half = D_MODEL // 2
    o_ref[...] = pltpu.pack_elementwise([t_ref[:, :half], t_ref[:, half:]], packed_dtype=BF16)


def _pack_table(tab, tm):
    n = tab.shape[0]
    packed = pl.pallas_call(
        _pack_kernel,
        grid=(n // tm,),
        in_specs=[pl.BlockSpec((tm, D_MODEL), lambda i: (i, 0))],
        out_specs=pl.BlockSpec((tm, D_MODEL // 2), lambda i: (i, 0)),
        out_shape=jax.ShapeDtypeStruct((n, D_MODEL // 2), jnp.uint32),
        compiler_params=pltpu.CompilerParams(dimension_semantics=("arbitrary",)),
        name="pack_table",
    )(tab)
    return packed.reshape(n * PACK_CHUNKS, LANES)


def _unpack(word):
    return (pltpu.unpack_elementwise(word, index=0, packed_dtype=BF16, unpacked_dtype=F32),
            pltpu.unpack_elementwise(word, index=1, packed_dtype=BF16, unpacked_dtype=F32))


def _gather_group(idx_ref, tab_ref, tile_ref):
    for r in range(PEER_ROWS):
        row = pl.multiple_of(idx_ref[r], PACK_CHUNKS)
        tile_ref[pl.ds(r * PACK_CHUNKS, PACK_CHUNKS), :] = tab_ref[pl.ds(row, PACK_CHUNKS), :]


def _tile_rows(tile_ref, c, j):
    start = j * PEER_GROUP * PACK_CHUNKS + c
    return tile_ref[pl.ds(start, PEER_GROUP, stride=PACK_CHUNKS), :]


def _index_copy(e_hbm, group, buf, sem, slot):
    return pltpu.make_async_copy(e_hbm.at[group], buf, sem.at[slot])


def _for_each_group(e_hbm, idx_bufs, sem, groups_per_step, process):
    step = pl.program_id(0)
    total = pl.num_programs(0) * groups_per_step
    base = step * groups_per_step

    @pl.when(step == 0)
    def _():
        for slot in range(2):
            _index_copy(e_hbm, slot, idx_bufs[slot], sem, slot).start()

    def pair(p, carry):
        for slot in range(2):
            local = 2 * p + slot
            group = base + local
            _index_copy(e_hbm, group, idx_bufs[slot], sem, slot).wait()
            process(idx_bufs[slot], local)

            @pl.when(group + 2 < total)
            def _():
                _index_copy(e_hbm, group + 2, idx_bufs[slot], sem, slot).start()
        return carry

    lax.fori_loop(0, groups_per_step // 2, pair, 0)


def _peer_u_kernel(tiles_a, e_hbm, xa_ref, xb_ref, g_ref, tab_ref, w_ref, x_ref, tile_ref,
                   idx_a, idx_b, sem):
    lane = lax.broadcasted_iota(I32, (PEER_GROUP, PEER_PICKS), 1)
    half = D_MODEL // 2
    x_ref[...] = _two_part_tile(tiles_a, xa_ref, xb_ref)

    def process(idx_ref, local):
        t0 = pl.multiple_of(local * PEER_GROUP, PEER_GROUP)
        _gather_group(idx_ref, tab_ref, tile_ref)
        xg = x_ref[pl.ds(t0, PEER_GROUP), :]
        a = jnp.zeros((PEER_GROUP, PEER_PICKS), F32)
        for j in range(PEER_PICKS):
            p = None
            for c in range(PACK_CHUNKS):
                lo, hi = _unpack(_tile_rows(tile_ref, c, j))
                term = (lo * xg[:, c * LANES:(c + 1) * LANES]
                        + hi * xg[:, half + c * LANES:half + (c + 1) * LANES])
                p = term if p is None else p + term
            a = jnp.where(lane == j, jnp.sum(p, axis=1, keepdims=True), a)
        w_ref[pl.ds(t0, PEER_GROUP), :] = g_ref[pl.ds(t0, PEER_GROUP), :] * jax.nn.gelu(a)

    _for_each_group(e_hbm, (idx_a, idx_b), sem, x_ref.shape[0] // PEER_GROUP, process)


def _peer_v_kernel(e_hbm, w_ref, tab_ref, f_ref, tile_ref, idx_a, idx_b, sem):
    def process(idx_ref, local):
        t0 = pl.multiple_of(local * PEER_GROUP, PEER_GROUP)
        _gather_group(idx_ref, tab_ref, tile_ref)
        wg = w_ref[pl.ds(t0, PEER_GROUP), :]
        acc = [jnp.zeros((PEER_GROUP, LANES), F32) for _ in range(2 * PACK_CHUNKS)]
        for j in range(PEER_PICKS):
            wb = jnp.broadcast_to(wg[:, j:j + 1], (PEER_GROUP, LANES))
            for c in range(PACK_CHUNKS):
                lo, hi = _unpack(_tile_rows(tile_ref, c, j))
                acc[c] = acc[c] + wb * lo
                acc[PACK_CHUNKS + c] = acc[PACK_CHUNKS + c] + wb * hi
        f_ref[pl.ds(t0, PEER_GROUP), :] = jnp.concatenate(acc, axis=1)

    _for_each_group(e_hbm, (idx_a, idx_b), sem, w_ref.shape[0] // PEER_GROUP, process)


def _peer_specs(tb):
    assert (tb // PEER_GROUP) % 2 == 0
    idx = pl.BlockSpec(memory_space=pl.ANY)
    picks = pl.BlockSpec((tb, PEER_PICKS), lambda i: (i, 0))
    feat = pl.BlockSpec((tb, D_MODEL), lambda i: (i, 0))
    table = pl.BlockSpec((N_EXPERTS * PACK_CHUNKS, LANES), lambda i: (0, 0),
                         pipeline_mode=pl.Buffered(1))
    scratch = [pltpu.VMEM((PACK_CHUNKS * PEER_ROWS, LANES), jnp.uint32),
               pltpu.SMEM((PEER_ROWS,), I32), pltpu.SMEM((PEER_ROWS,), I32),
               pltpu.SemaphoreType.DMA((2,))]
    params = pltpu.CompilerParams(dimension_semantics=("arbitrary",),
                                  vmem_limit_bytes=48 * MIB)
    return idx, picks, feat, table, scratch, params


def _peer_u(e_grp, x1a, x1b, g, tab_u, tb):
    T = x1a.shape[0] + x1b.shape[0]
    tiles_a = x1a.shape[0] // tb
    idx, picks, feat, table, scratch, params = _peer_specs(tb)
    return pl.pallas_call(
        functools.partial(_peer_u_kernel, tiles_a), grid=(T // tb,),
        in_specs=[idx, *_two_part_specs(tiles_a, tb, D_MODEL), picks, table], out_specs=picks,
        out_shape=jax.ShapeDtypeStruct((T, PEER_PICKS), F32),
        scratch_shapes=[pltpu.VMEM((tb, D_MODEL), F32)] + scratch,
        compiler_params=params,
        name="peer_u",
    )(e_grp, x1a, x1b, g, tab_u)


def _peer_v(e_grp, w, tab_v, tb):
    T = w.shape[0]
    idx, picks, feat, table, scratch, params = _peer_specs(tb)
    return pl.pallas_call(
        _peer_v_kernel, grid=(T // tb,),
        in_specs=[idx, picks, table], out_specs=feat,
        out_shape=jax.ShapeDtypeStruct((T, D_MODEL), F32),
        scratch_shapes=scratch,
        compiler_params=params, name="peer_v",
    )(e_grp, w, tab_v)


def _final_kernel(x1_ref, f_ref, p_ref, wg_ref, wp_ref, g_ref, b_ref, y_ref):
    x1 = x1_ref[...]
    e = jax.nn.sigmoid(_dot(x1, wg_ref[...])) * _dot(p_ref[...], wp_ref[...])
    y_ref[...] = _layer_norm(ALPHA * x1 + f_ref[...] + e, g_ref[...], b_ref[...])


def _final(x1, f, f_row_offset, p, wg, wp, ln_g, ln_b, tm):
    T = p.shape[0]
    off = f_row_offset // tm
    full = lambda a: pl.BlockSpec(a.shape, lambda i: (0,) * a.ndim)
    shifted = pl.BlockSpec((tm, D_MODEL), lambda i: (i + off, 0))
    return pl.pallas_call(
        _final_kernel,
        grid=(T // tm,),
        in_specs=[pl.BlockSpec((tm, D_MODEL), lambda i: (i, 0)), shifted,
                  pl.BlockSpec((tm, PLE_DIM), lambda i: (i, 0)),
                  full(wg), full(wp), full(ln_g), full(ln_b)],
        out_specs=pl.BlockSpec((tm, D_MODEL), lambda i: (i, 0)),
        out_shape=jax.ShapeDtypeStruct((T, D_MODEL), F32),
        compiler_params=pltpu.CompilerParams(
            dimension_semantics=("arbitrary",), vmem_limit_bytes=48 * MIB),
        name="final",
    )(x1, f, p, wg, wp, ln_g, ln_b)


def _t5_bias_by_distance(dist, table):
    n = jnp.maximum(dist, 0)
    max_exact = N_BUCKETS // 2
    nf = jnp.maximum(n, 1).astype(F32)
    large = max_exact + jnp.floor(jnp.log(nf / max_exact) / math.log(MAX_DISTANCE / max_exact)
                                  * (N_BUCKETS - max_exact)).astype(I32)
    large = jnp.minimum(large, N_BUCKETS - 1)
    bucket = jnp.where(n < max_exact, n, large)
    hit = bucket[None, :, None] == jnp.arange(N_BUCKETS)
    return jnp.sum(jnp.where(hit, table.T[:, None, :], 0.0), axis=-1)


def _band_bias(table, n_query, n_key, key_offset):
    d_max = n_query - 1 + key_offset
    d_min = key_offset - (n_key - 1)
    by_dist = _t5_bias_by_distance(jnp.arange(d_max, d_min - 1, -1), table)
    rows = [by_dist[:, d_max - (q + key_offset):d_max - (q + key_offset) + n_key]
            for q in range(n_query)]
    return jnp.stack(rows, axis=1)


def kernel(x_prompt, x_sample, cache_k_win, cache_v_win, p_prompt, p_sample, rel_bias_table,
           w_in, attn_sinks, w_att_out, c_ln_g, c_ln_b, c_ws, c_bs, w_chunk_out, w_o, ln1_g,
           ln1_b, peer_wq, peer_subkeys, peer_u, peer_v, w_ple_gate, w_ple_proj, ln2_g, ln2_b):
    batch, seq, d = x_prompt.shape
    n_seq, s_new, _ = x_sample.shape
    w_buf = cache_k_win.shape[2]
    assert d == D_MODEL and w_in.shape[0] == DEPTH and seq % ATT_BLOCK == 0
    tp = batch * seq
    ts = n_seq * s_new
    t_all = tp + ts
    nb = seq // ATT_BLOCK
    seqs_per_step = 8
    assert n_seq % seqs_per_step == 0 and tp % (seqs_per_step * s_new) == 0
    assert tp % 256 == 0 and ts % 256 == 0 and tp % PEER_TOKENS_PER_STEP == 0
    assert tp % ROUTE_TILE == 0 and ts % ROUTE_TILE == 0

    row2 = lambda v: v.reshape(1, -1)
    w_in_b = w_in[0].astype(BF16)
    wa, wc, wo = w_att_out[0].astype(BF16), w_chunk_out[0].astype(BF16), w_o[0].astype(BF16)
    sinks = attn_sinks[0]
    cg, cb = row2(c_ln_g[0]), row2(c_ln_b[0])
    l1g, l1b = row2(ln1_g[0]), row2(ln1_b[0])
    l2g, l2b = row2(ln2_g[0]), row2(ln2_b[0])
    ws, bs = c_ws[0], c_bs[0]

    bias_p = _band_bias(rel_bias_table, ATT_BLOCK, 2 * ATT_BLOCK, ATT_BLOCK)
    bias_s = _band_bias(rel_bias_table, s_new, w_buf + s_new, w_buf)
    bias_s = bias_s.reshape(N_KV, REP * s_new, w_buf + s_new)
    bias_sc, bias_sn = bias_s[:, :, :w_buf], bias_s[:, :, w_buf:]

    bs_exp_p = jnp.repeat(bs.T, C_GDIM, axis=1)
    bs_exp_s = jnp.tile(jnp.repeat(bs[:, :s_new].T, C_GDIM, axis=1), (seqs_per_step, 1))
    tril = jnp.tril(jnp.ones((s_new, s_new), F32))
    eye = jnp.eye(seqs_per_step, dtype=F32)
    wbd = jnp.stack([jnp.kron(eye, ws[g, :s_new, :s_new] * tril) for g in range(C_GROUPS)])

    xp = x_prompt.reshape(tp, d)
    q, kv, cu, cvn, gate = _inproj(xp, w_in_b, cg, cb, 256)
    x1p = _prompt_blocks(nb, sinks, q, kv, cu, cvn, gate, xp, bias_p, ws, bs_exp_p,
                        wa, wc, wo, l1g, l1b)
    kv_tail = kv.reshape(batch, seq, 2 * KV_WIDTH)[:, seq - w_buf:]
    kp = kv_tail[..., :KV_WIDTH].reshape(1, batch, w_buf, N_KV, HEAD_DIM)
    vp = kv_tail[..., KV_WIDTH:].reshape(1, batch, w_buf, N_KV, HEAD_DIM)

    xs = x_sample.reshape(ts, d)
    q, kv, cu, cvn_s, gate = _inproj(xs, w_in_b, cg, cb, 256)
    x1s, k_new, v_new = _sample_blocks(
        seqs_per_step, s_new, sinks, q, kv, cu, cvn_s, gate, xs,
        cache_k_win[0].reshape(n_seq, w_buf, KV_WIDTH),
        cache_v_win[0].reshape(n_seq, w_buf, KV_WIDTH),
        bias_sc, bias_sn, wbd, bs_exp_s, wa, wc, wo, l1g, l1b)
    ks_out = k_new.reshape(1, n_seq, w_buf, N_KV, HEAD_DIM)
    vs_out = v_new.reshape(1, n_seq, w_buf, N_KV, HEAD_DIM)
    cs_out = cvn_s.reshape(1, n_seq, s_new, C_WIDTH)

    e_t, g_t = _route(x1p, x1s, peer_wq[0].astype(BF16), peer_subkeys[0], ROUTE_TILE)
    nt = t_all // ROUTE_TILE
    e_grp = (e_t.reshape(nt, PEER_PICKS, ROUTE_TILE // PEER_GROUP, PEER_GROUP)
             .transpose(0, 2, 1, 3).reshape(t_all // PEER_GROUP, PEER_ROWS))
    g_tok = g_t.transpose(0, 2, 1).reshape(t_all, PEER_PICKS)
    tab_u = _pack_table(peer_u[0], 256)
    tab_v = _pack_table(peer_v[0], 256)
    w_tok = _peer_u(e_grp, x1p, x1s, g_tok, tab_u, PEER_TOKENS_PER_STEP)
    f = _peer_v(e_grp, w_tok, tab_v, PEER_TOKENS_PER_STEP)

    wg, wp = w_ple_gate[0].astype(BF16), w_ple_proj[0].astype(BF16)
    yp = _final(x1p, f, 0, p_prompt[0].reshape(tp, PLE_DIM), wg, wp, l2g, l2b,
                512 if tp % 512 == 0 else 256)
    ys = _final(x1s, f, tp, p_sample[0].reshape(ts, PLE_DIM), wg, wp, l2g, l2b, 256)
    return (yp.reshape(batch, seq, d), ys.reshape(n_seq, s_new, d), kp, vp, ks_out, vs_out,
            cs_out)
```

```python
import functools
import math

import jax
import jax.numpy as jnp
from jax import lax
from jax.experimental import pallas as pl
from jax.experimental.pallas import tpu as pltpu

F32 = jnp.float32
BF16 = jnp.bfloat16
I32 = jnp.int32

D_MODEL = 1024
N_HEADS = 8
N_KV = 2
REP = N_HEADS // N_KV
HEAD_DIM = 64
Q_WIDTH = N_HEADS * HEAD_DIM
KV_WIDTH = N_KV * HEAD_DIM
WINDOW = 128
ATT_BLOCK = 128
N_BUCKETS = 32
MAX_DISTANCE = 128
C_GROUPS = 4
C_CHUNK = 128
C_WIDTH = 512
C_GDIM = C_WIDTH // C_GROUPS
PLE_DIM = 256
PEER_HEADS = 8
N_KEYS = 128
N_EXPERTS = N_KEYS * N_KEYS
PEER_TOPK = 16
PEER_DKEY = 256
PEER_DHALF = PEER_DKEY // 2
PEER_PICKS = PEER_HEADS * PEER_TOPK
DEPTH = 1
ALPHA = (2.0 * DEPTH) ** 0.25
LN_EPS = 1e-5
NEG = -1e30
IN_WIDTH = Q_WIDTH + 2 * KV_WIDTH + 2 * C_WIDTH + 2 * D_MODEL

LANES = 128
SUBLANES = 8
MIB = 1024 * 1024

PEER_GROUP = SUBLANES
PEER_ROWS = PEER_GROUP * PEER_PICKS
PACK_CHUNKS = D_MODEL // (2 * LANES)
PEER_TOKENS_PER_STEP = 256
ROUTE_TILE = 512


def _layer_norm(x, g, b):
    mu = jnp.mean(x, axis=-1, keepdims=True)
    xc = x - mu
    var = jnp.mean(xc * xc, axis=-1, keepdims=True)
    return xc * lax.rsqrt(var + LN_EPS) * g + b


def _dot(a, b):
    return jnp.dot(a.astype(BF16), b.astype(BF16), preferred_element_type=F32)


def _dot_nt(a, b):
    return lax.dot_general(a.astype(BF16), b.astype(BF16), (((1,), (1,)), ((), ())),
                           preferred_element_type=F32)


def _inproj_kernel(x_ref, w_ref, g_ref, b_ref, q_ref, kv_ref, cu_ref, cvn_ref, gate_ref):
    z = jnp.dot(x_ref[...].astype(BF16), w_ref[...], preferred_element_type=F32)
    o = 0
    q_ref[...] = z[:, o:o + Q_WIDTH]
    o += Q_WIDTH
    kv_ref[...] = z[:, o:o + 2 * KV_WIDTH]
    o += 2 * KV_WIDTH
    cu_ref[...] = z[:, o:o + C_WIDTH]
    o += C_WIDTH
    cvn_ref[...] = _layer_norm(z[:, o:o + C_WIDTH], g_ref[...], b_ref[...])
    o += C_WIDTH
    gate_ref[...] = z[:, o:o + 2 * D_MODEL]


def _inproj(x, w_bf16, c_g, c_b, tm):
    T = x.shape[0]
    widths = (Q_WIDTH, 2 * KV_WIDTH, C_WIDTH, C_WIDTH, 2 * D_MODEL)
    row = lambda n: pl.BlockSpec((tm, n), lambda i: (i, 0))
    full = lambda a: pl.BlockSpec(a.shape, lambda i: (0,) * a.ndim)
    return pl.pallas_call(
        _inproj_kernel,
        grid=(T // tm,),
        in_specs=[row(D_MODEL), full(w_bf16), full(c_g), full(c_b)],
        out_specs=[row(n) for n in widths],
        out_shape=[jax.ShapeDtypeStruct((T, n), F32) for n in widths],
        compiler_params=pltpu.CompilerParams(
            dimension_semantics=("arbitrary",), vmem_limit_bytes=48 * MIB),
        name="inproj",
    )(x, w_bf16, c_g, c_b)


def _merge_and_norm(x, ya, yc, gate, wa_ref, wc_ref, wo_ref, g_ref, b_ref):
    ga = gate[:, :D_MODEL]
    gc = gate[:, D_MODEL:]
    mix = jax.nn.sigmoid(ga) * _dot(ya, wa_ref[...]) + jax.nn.sigmoid(gc) * _dot(yc, wc_ref[...])
    h = ALPHA * x + _dot(mix, wo_ref[...])
    return _layer_norm(h, g_ref[...], b_ref[...])


def _softmax_with_sink(s, sink):
    m = jnp.maximum(jnp.max(s, axis=-1, keepdims=True), sink)
    e = jnp.exp(s - m)
    return e / (jnp.sum(e, axis=-1, keepdims=True) + jnp.exp(sink - m))


PROMPT_BLOCKS_PER_STEP = 2


def _prompt_block_kernel(nb, sinks_ref, q_ref, kvp_ref, kvo_ref, cu_ref, cvn_ref, gate_ref,
                         x_ref, bias_ref, ws_ref, bs_ref, wa_ref, wc_ref, wo_ref, g_ref,
                         b_ref, x1_ref):
    blk = ATT_BLOCK
    qi = lax.broadcasted_iota(I32, (blk, 2 * blk), 0)
    kj = lax.broadcasted_iota(I32, (blk, 2 * blk), 1)
    dist = qi - kj + blk
    in_window = (dist >= 0) & (dist < WINDOW)
    ti = lax.broadcasted_iota(I32, (C_CHUNK, C_CHUNK), 0)
    si = lax.broadcasted_iota(I32, (C_CHUNK, C_CHUNK), 1)
    wms = [jnp.where(si <= ti, ws_ref[g], 0.0) for g in range(C_GROUPS)]

    scale = HEAD_DIM ** -0.5
    work = [(sb, h) for sb in range(PROMPT_BLOCKS_PER_STEP) for h in range(N_HEADS)]
    kvs, scores = {}, {}
    for sb in range(PROMPT_BLOCKS_PER_STEP):
        rows = slice(sb * blk, (sb + 1) * blk)
        kv_prev = kvp_ref[...] if sb == 0 else kvo_ref[(sb - 1) * blk:sb * blk, :]
        kvs[sb] = jnp.concatenate([kv_prev, kvo_ref[rows, :]], axis=0)
    for sb, h in work:
        n = lax.rem(pl.program_id(0) * PROMPT_BLOCKS_PER_STEP + sb, nb)
        mask = in_window & ((kj >= blk) | (n > 0))
        g = h // REP
        qh = q_ref[sb * blk:(sb + 1) * blk, h * HEAD_DIM:(h + 1) * HEAD_DIM]
        kg = kvs[sb][:, g * HEAD_DIM:(g + 1) * HEAD_DIM]
        scores[sb, h] = jnp.where(mask, _dot_nt(qh, kg) * scale + bias_ref[h], NEG)
    probs = {key: _softmax_with_sink(scores[key], sinks_ref[key[1]]) for key in work}
    ya_blocks, yc_blocks = [], []
    for sb in range(PROMPT_BLOCKS_PER_STEP):
        rows = slice(sb * blk, (sb + 1) * blk)
        outs = []
        for h in range(N_HEADS):
            g = h // REP
            vg = kvs[sb][:, KV_WIDTH + g * HEAD_DIM:KV_WIDTH + (g + 1) * HEAD_DIM]
            outs.append(_dot(probs[sb, h], vg))
        ya_blocks.append(jnp.concatenate(outs, axis=1))

        cvn = cvn_ref[rows, :]
        parts = [_dot(wms[g], cvn[:, g * C_GDIM:(g + 1) * C_GDIM]) for g in range(C_GROUPS)]
        yc_blocks.append(cu_ref[rows, :] * (jnp.concatenate(parts, axis=1) + bs_ref[...]))

    ya = jnp.concatenate(ya_blocks, axis=0)
    yc = jnp.concatenate(yc_blocks, axis=0)
    x1_ref[...] = _merge_and_norm(x_ref[...], ya, yc, gate_ref[...], wa_ref, wc_ref, wo_ref,
                                  g_ref, b_ref)


def _prompt_blocks(nb, sinks, q, kv, cu, cvn, gate, x, bias, ws, bs_exp,
                   wa, wc, wo, ln_g, ln_b):
    T = q.shape[0]
    blk = ATT_BLOCK
    tm = PROMPT_BLOCKS_PER_STEP * blk
    assert nb % PROMPT_BLOCKS_PER_STEP == 0
    row = lambda n: pl.BlockSpec((tm, n), lambda i: (i, 0))
    full = lambda a: pl.BlockSpec(a.shape, lambda i: (0,) * a.ndim)
    prev = pl.BlockSpec((blk, 2 * KV_WIDTH),
                        lambda i: (jnp.maximum(i * PROMPT_BLOCKS_PER_STEP - 1, 0), 0))
    return pl.pallas_call(
        functools.partial(_prompt_block_kernel, nb),
        grid=(T // tm,),
        in_specs=[pl.BlockSpec(memory_space=pltpu.SMEM),
                  row(Q_WIDTH), prev, row(2 * KV_WIDTH), row(C_WIDTH), row(C_WIDTH),
                  row(2 * D_MODEL), row(D_MODEL), full(bias), full(ws), full(bs_exp),
                  full(wa), full(wc), full(wo), full(ln_g), full(ln_b)],
        out_specs=row(D_MODEL),
        out_shape=jax.ShapeDtypeStruct((T, D_MODEL), F32),
        compiler_params=pltpu.CompilerParams(
            dimension_semantics=("arbitrary",), vmem_limit_bytes=48 * MIB),
        name="prompt_blocks",
    )(sinks, q, kv, kv, cu, cvn, gate, x, bias, ws, bs_exp, wa, wc, wo, ln_g, ln_b)


def _sample_block_kernel(seqs, s_new, sinks_ref, q_ref, kv_ref, cu_ref, cvn_ref, gate_ref,
                         x_ref, ck_ref, cv_ref, biasc_ref, biasn_ref, wbd_ref, bs_ref, wa_ref,
                         wc_ref, wo_ref, g_ref, b_ref, x1_ref, kout_ref, vout_ref):
    w_buf = ck_ref.shape[1]
    rows = REP * s_new
    qi_c = lax.rem(lax.broadcasted_iota(I32, (rows, w_buf), 0), s_new)
    kj_c = lax.broadcasted_iota(I32, (rows, w_buf), 1)
    mask_c = (qi_c + w_buf - kj_c) < WINDOW
    qi_n = lax.rem(lax.broadcasted_iota(I32, (rows, s_new), 0), s_new)
    kj_n = lax.broadcasted_iota(I32, (rows, s_new), 1)
    mask_n = kj_n <= qi_n
    sink_col = [jnp.concatenate([jnp.full((s_new, 1), sinks_ref[g * REP + r], F32)
                                 for r in range(REP)], axis=0) for g in range(N_KV)]

    q_all = q_ref[...]
    kv_all = kv_ref[...]
    scale = HEAD_DIM ** -0.5
    work = [(b, g) for b in range(seqs) for g in range(N_KV)]
    ck, cv, kvb, scores = {}, {}, {}, {}
    for b in range(seqs):
        kvb[b] = kv_all[b * s_new:(b + 1) * s_new]
        ck[b] = ck_ref[b]
        cv[b] = cv_ref[b]
        kout_ref[b] = jnp.concatenate([ck[b][s_new:], kvb[b][:, :KV_WIDTH]], axis=0)
        vout_ref[b] = jnp.concatenate([cv[b][s_new:], kvb[b][:, KV_WIDTH:]], axis=0)
    for b, g in work:
        qb = q_all[b * s_new:(b + 1) * s_new]
        qg = jnp.concatenate([qb[:, (g * REP + r) * HEAD_DIM:(g * REP + r + 1) * HEAD_DIM]
                              for r in range(REP)], axis=0)
        lane = slice(g * HEAD_DIM, (g + 1) * HEAD_DIM)
        sc = _dot_nt(qg, ck[b][:, lane]) * scale + biasc_ref[g]
        sn = _dot_nt(qg, kvb[b][:, lane]) * scale + biasn_ref[g]
        scores[b, g] = (jnp.where(mask_c, sc, NEG), jnp.where(mask_n, sn, NEG))
    probs = {}
    for b, g in work:
        sc, sn = scores[b, g]
        sink = sink_col[g]
        m = jnp.maximum(jnp.maximum(jnp.max(sc, axis=-1, keepdims=True),
                                    jnp.max(sn, axis=-1, keepdims=True)), sink)
        ec = jnp.exp(sc - m)
        en = jnp.exp(sn - m)
        den = (jnp.sum(ec, axis=-1, keepdims=True) + jnp.sum(en, axis=-1, keepdims=True)
               + jnp.exp(sink - m))
        probs[b, g] = (ec / den, en / den)
    ya_rows = []
    for b in range(seqs):
        heads = []
        for g in range(N_KV):
            lane = slice(g * HEAD_DIM, (g + 1) * HEAD_DIM)
            vlane = slice(KV_WIDTH + g * HEAD_DIM, KV_WIDTH + (g + 1) * HEAD_DIM)
            pc, pn = probs[b, g]
            og = _dot(pc, cv[b][:, lane]) + _dot(pn, kvb[b][:, vlane])
            heads.extend(og[r * s_new:(r + 1) * s_new] for r in range(REP))
        ya_rows.append(jnp.concatenate(heads, axis=1))
    ya = jnp.concatenate(ya_rows, axis=0)

    cvn = cvn_ref[...]
    parts = [_dot(wbd_ref[g], cvn[:, g * C_GDIM:(g + 1) * C_GDIM]) for g in range(C_GROUPS)]
    yc = cu_ref[...] * (jnp.concatenate(parts, axis=1) + bs_ref[...])

    x1_ref[...] = _merge_and_norm(x_ref[...], ya, yc, gate_ref[...], wa_ref, wc_ref, wo_ref,
                                  g_ref, b_ref)


def _sample_blocks(seqs, s_new, sinks, q, kv, cu, cvn, gate, x, cache_k,
                   cache_v, bias_c, bias_n, wbd, bs_exp, wa, wc, wo, ln_g, ln_b):
    T = q.shape[0]
    n_seq, w_buf, kvw = cache_k.shape
    tm = seqs * s_new
    row = lambda n: pl.BlockSpec((tm, n), lambda i: (i, 0))
    full = lambda a: pl.BlockSpec(a.shape, lambda i: (0,) * a.ndim)
    cache = pl.BlockSpec((seqs, w_buf, kvw), lambda i: (i, 0, 0))
    return pl.pallas_call(
        functools.partial(_sample_block_kernel, seqs, s_new),
        grid=(T // tm,),
        in_specs=[pl.BlockSpec(memory_space=pltpu.SMEM),
                  row(Q_WIDTH), row(2 * KV_WIDTH), row(C_WIDTH), row(C_WIDTH),
                  row(2 * D_MODEL), row(D_MODEL), cache, cache, full(bias_c), full(bias_n),
                  full(wbd), full(bs_exp), full(wa), full(wc), full(wo), full(ln_g),
                  full(ln_b)],
        out_specs=[row(D_MODEL), cache, cache],
        out_shape=[jax.ShapeDtypeStruct((T, D_MODEL), F32),
                   jax.ShapeDtypeStruct(cache_k.shape, F32),
                   jax.ShapeDtypeStruct(cache_v.shape, F32)],
        compiler_params=pltpu.CompilerParams(
            dimension_semantics=("arbitrary",), vmem_limit_bytes=48 * MIB),
        name="sample_blocks",
    )(sinks, q, kv, cu, cvn, gate, x, cache_k, cache_v, bias_c, bias_n, wbd, bs_exp, wa, wc,
      wo, ln_g, ln_b)


def _two_part_specs(tiles_a, tile_rows, width):
    first = pl.BlockSpec((tile_rows, width), lambda i: (jnp.minimum(i, tiles_a - 1), 0))
    second = pl.BlockSpec((tile_rows, width), lambda i: (jnp.maximum(i - tiles_a, 0), 0))
    return first, second


def _two_part_tile(tiles_a, a_ref, b_ref):
    return jnp.where(pl.program_id(0) < tiles_a, a_ref[...], b_ref[...])


def _top16(s, iota, fill):
    vals, idxs = [], []
    for _ in range(PEER_TOPK):
        m = jnp.max(s, axis=0, keepdims=True)
        i = jnp.min(jnp.where(s == m, iota, fill), axis=0, keepdims=True)
        vals.append(m)
        idxs.append(i)
        s = jnp.where(iota == i, -jnp.inf, s)
    return jnp.concatenate(vals, axis=0), jnp.concatenate(idxs, axis=0)


def _pick(table, iota, idx):
    return jnp.concatenate(
        [jnp.sum(jnp.where(iota == idx[r:r + 1], table, 0), axis=0, keepdims=True)
         for r in range(PEER_TOPK)], axis=0)


_CAND_SHORT = SUBLANES


def _cand_flat_index(tt):
    i16 = lax.broadcasted_iota(I32, (PEER_TOPK, tt), 0)
    i8 = lax.broadcasted_iota(I32, (_CAND_SHORT, tt), 0)
    pieces = [i16] + [k * PEER_TOPK + i8 for k in range(1, _CAND_SHORT)]
    pieces.append((_CAND_SHORT + i8) * PEER_TOPK)
    return jnp.concatenate(pieces, axis=0)


def _cand_values(sv0, sv1):
    pieces = [sv0[0:1] + sv1]
    pieces += [sv0[k:k + 1] + sv1[0:_CAND_SHORT] for k in range(1, _CAND_SHORT)]
    pieces.append(sv0[_CAND_SHORT:] + sv1[0:1])
    return jnp.concatenate(pieces, axis=0)


def _route_kernel(tiles_a, xa_ref, xb_ref, wq_ref, sk_ref, e_ref, g_ref, q_scr):
    tt = xa_ref.shape[0]
    x1 = _two_part_tile(tiles_a, xa_ref, xb_ref)
    qf = jnp.dot(x1.astype(BF16), wq_ref[...], preferred_element_type=F32)
    for hc in range(2 * PEER_HEADS):
        q_scr[hc] = qf[:, hc * PEER_DHALF:(hc + 1) * PEER_DHALF]
    iota_n = lax.broadcasted_iota(I32, (N_KEYS, tt), 0)
    iota_k = lax.broadcasted_iota(I32, (PEER_TOPK, tt), 0)
    flat = _cand_flat_index(tt)

    def first_stage(h):
        out = []
        for c in range(2):
            s = _dot_nt(sk_ref[h, c], q_scr[2 * h + c])
            out.extend(_top16(s, iota_n, N_KEYS))
        return out[0], out[2], out[1], out[3]

    def second_stage(h, carry):
        sv0, sv1, si0, si1 = carry
        fv, fi = _top16(_cand_values(sv0, sv1), flat, PEER_TOPK * PEER_TOPK)
        i1 = _pick(si0, iota_k, lax.shift_right_logical(fi, 4))
        i2 = _pick(si1, iota_k, lax.bitwise_and(fi, PEER_TOPK - 1))
        ex = jnp.exp(fv - fv[0:1])
        row0 = pl.multiple_of(h * PEER_TOPK, PEER_TOPK)
        e_ref[0, pl.ds(row0, PEER_TOPK), :] = (i1 * N_KEYS + i2) * PACK_CHUNKS
        g_ref[0, pl.ds(row0, PEER_TOPK), :] = ex / jnp.sum(ex, axis=0, keepdims=True)

    def body(h, carry):
        second_stage(h - 1, carry)
        return first_stage(h)

    last = lax.fori_loop(1, PEER_HEADS, body, first_stage(0))
    second_stage(PEER_HEADS - 1, last)


def _route(x1a, x1b, wq_bf16, subkeys, tt):
    T = x1a.shape[0] + x1b.shape[0]
    tiles_a = x1a.shape[0] // tt
    full = lambda a: pl.BlockSpec(a.shape, lambda i: (0,) * a.ndim)
    out = pl.BlockSpec((1, PEER_PICKS, tt), lambda i: (i, 0, 0))
    return pl.pallas_call(
        functools.partial(_route_kernel, tiles_a),
        grid=(T // tt,),
        in_specs=[*_two_part_specs(tiles_a, tt, D_MODEL), full(wq_bf16), full(subkeys)],
        out_specs=[out, out],
        out_shape=[jax.ShapeDtypeStruct((T // tt, PEER_PICKS, tt), I32),
                   jax.ShapeDtypeStruct((T // tt, PEER_PICKS, tt), F32)],
        scratch_shapes=[pltpu.VMEM((2 * PEER_HEADS, tt, PEER_DHALF), F32)],
        compiler_params=pltpu.CompilerParams(
            dimension_semantics=("arbitrary",), vmem_limit_bytes=48 * MIB),
        name="peer_route",
    )(x1a, x1b, wq_bf16, subkeys)


def _pack_kernel(t_ref, o_ref):
    half = D_MODEL // 2
    o_ref[...] = pltpu.pack_elementwise([t_ref[:, :half], t_ref[:, half:]], packed_dtype=BF16)


def _pack_table(tab, tm):
    n = tab.shape[0]
    packed = pl.pallas_call(
        _pack_kernel,
        grid=(n // tm,),
        in_specs=[pl.BlockSpec((tm, D_MODEL), lambda i: (i, 0))],
        out_specs=pl.BlockSpec((tm, D_MODEL // 2), lambda i: (i, 0)),
        out_shape=jax.ShapeDtypeStruct((n, D_MODEL // 2), jnp.uint32),
        compiler_params=pltpu.CompilerParams(dimension_semantics=("arbitrary",)),
        name="pack_table",
    )(tab)
    return packed.reshape(n * PACK_CHUNKS, LANES)


def _unpack(word):
    return (pltpu.unpack_elementwise(word, index=0, packed_dtype=BF16, unpacked_dtype=F32),
            pltpu.unpack_elementwise(word, index=1, packed_dtype=BF16, unpacked_dtype=F32))


def _gather_group(idx_ref, tab_ref, tile_ref):
    for r in range(PEER_ROWS):
        row = pl.multiple_of(idx_ref[r], PACK_CHUNKS)
        tile_ref[pl.ds(r * PACK_CHUNKS, PACK_CHUNKS), :] = tab_ref[pl.ds(row, PACK_CHUNKS), :]


def _tile_rows(tile_ref, c, j):
    start = j * PEER_GROUP * PACK_CHUNKS + c
    return tile_ref[pl.ds(start, PEER_GROUP, stride=PACK_CHUNKS), :]


def _index_copy(e_hbm, group, buf, sem, slot):
    return pltpu.make_async_copy(e_hbm.at[group], buf, sem.at[slot])


def _for_each_group(e_hbm, idx_bufs, sem, groups_per_step, process, groups_per_block=1):
    per_iter = 2 * groups_per_block
    assert len(idx_bufs) == per_iter and groups_per_step % per_iter == 0
    step = pl.program_id(0)
    total = pl.num_programs(0) * groups_per_step
    base = step * groups_per_step

    @pl.when(step == 0)
    def _():
        for slot in range(per_iter):
            _index_copy(e_hbm, slot, idx_bufs[slot], sem, slot).start()

    def body(it, carry):
        first = base + per_iter * it
        for b in range(2):
            slots = range(b * groups_per_block, (b + 1) * groups_per_block)
            for slot in slots:
                _index_copy(e_hbm, first + slot, idx_bufs[slot], sem, slot).wait()
            for k, slot in enumerate(slots):
                process(idx_bufs[slot], per_iter * it + slot, k)

            @pl.when(first + per_iter + slots[0] < total)
            def _():
                for slot in slots:
                    _index_copy(e_hbm, first + per_iter + slot, idx_bufs[slot], sem, slot).start()
        return carry

    lax.fori_loop(0, groups_per_step // per_iter, body, 0)


U_GROUPS_PER_BLOCK = 2


def _peer_u_kernel(tiles_a, e_hbm, xa_ref, xb_ref, g_ref, tab_ref, w_ref, x_ref, *scratch):
    tiles = scratch[:U_GROUPS_PER_BLOCK]
    idx_bufs, sem = scratch[U_GROUPS_PER_BLOCK:-1], scratch[-1]
    lane = lax.broadcasted_iota(I32, (PEER_GROUP, PEER_PICKS), 1)
    half = D_MODEL // 2
    x_ref[...] = _two_part_tile(tiles_a, xa_ref, xb_ref)

    def process(idx_ref, local, k):
        tile_ref = tiles[k]
        t0 = pl.multiple_of(local * PEER_GROUP, PEER_GROUP)
        _gather_group(idx_ref, tab_ref, tile_ref)
        xg = x_ref[pl.ds(t0, PEER_GROUP), :]
        x_lo = [xg[:, c * LANES:(c + 1) * LANES] for c in range(PACK_CHUNKS)]
        x_hi = [xg[:, half + c * LANES:half + (c + 1) * LANES] for c in range(PACK_CHUNKS)]
        a = jnp.zeros((PEER_GROUP, PEER_PICKS), F32)
        for j in range(PEER_PICKS):
            p = None
            for c in range(PACK_CHUNKS):
                lo, hi = _unpack(_tile_rows(tile_ref, c, j))
                term = lo * x_lo[c] + hi * x_hi[c]
                p = term if p is None else p + term
            a = jnp.where(lane == j, jnp.sum(p, axis=1, keepdims=True), a)
        w_ref[pl.ds(t0, PEER_GROUP), :] = g_ref[pl.ds(t0, PEER_GROUP), :] * jax.nn.gelu(a)

    _for_each_group(e_hbm, idx_bufs, sem, x_ref.shape[0] // PEER_GROUP, process,
                    U_GROUPS_PER_BLOCK)


def _peer_v_kernel(e_hbm, w_ref, tab_ref, f_ref, tile_ref, idx_a, idx_b, sem):
    def process(idx_ref, local, k):
        t0 = pl.multiple_of(local * PEER_GROUP, PEER_GROUP)
        _gather_group(idx_ref, tab_ref, tile_ref)
        wg = w_ref[pl.ds(t0, PEER_GROUP), :]
        acc = [jnp.zeros((PEER_GROUP, LANES), F32) for _ in range(2 * PACK_CHUNKS)]
        for j in range(PEER_PICKS):
            wb = jnp.broadcast_to(wg[:, j:j + 1], (PEER_GROUP, LANES))
            for c in range(PACK_CHUNKS):
                lo, hi = _unpack(_tile_rows(tile_ref, c, j))
                acc[c] = acc[c] + wb * lo
                acc[PACK_CHUNKS + c] = acc[PACK_CHUNKS + c] + wb * hi
        f_ref[pl.ds(t0, PEER_GROUP), :] = jnp.concatenate(acc, axis=1)

    _for_each_group(e_hbm, (idx_a, idx_b), sem, w_ref.shape[0] // PEER_GROUP, process)


def _peer_specs(tb, groups_per_block):
    idx = pl.BlockSpec(memory_space=pl.ANY)
    picks = pl.BlockSpec((tb, PEER_PICKS), lambda i: (i, 0))
    feat = pl.BlockSpec((tb, D_MODEL), lambda i: (i, 0))
    table = pl.BlockSpec((N_EXPERTS * PACK_CHUNKS, LANES), lambda i: (0, 0),
                         pipeline_mode=pl.Buffered(1))
    n_idx = 2 * groups_per_block
    scratch = ([pltpu.VMEM((PACK_CHUNKS * PEER_ROWS, LANES), jnp.uint32)] * groups_per_block
               + [pltpu.SMEM((PEER_ROWS,), I32)] * n_idx + [pltpu.SemaphoreType.DMA((n_idx,))])
    params = pltpu.CompilerParams(dimension_semantics=("arbitrary",),
                                  vmem_limit_bytes=48 * MIB)
    return idx, picks, feat, table, scratch, params


def _peer_u(e_grp, x1a, x1b, g, tab_u, tb):
    T = x1a.shape[0] + x1b.shape[0]
    tiles_a = x1a.shape[0] // tb
    idx, picks, feat, table, scratch, params = _peer_specs(tb, U_GROUPS_PER_BLOCK)
    return pl.pallas_call(
        functools.partial(_peer_u_kernel, tiles_a), grid=(T // tb,),
        in_specs=[idx, *_two_part_specs(tiles_a, tb, D_MODEL), picks, table], out_specs=picks,
        out_shape=jax.ShapeDtypeStruct((T, PEER_PICKS), F32),
        scratch_shapes=[pltpu.VMEM((tb, D_MODEL), F32)] + scratch,
        compiler_params=params,
        name="peer_u",
    )(e_grp, x1a, x1b, g, tab_u)


def _peer_v(e_grp, w, tab_v, tb):
    T = w.shape[0]
    idx, picks, feat, table, scratch, params = _peer_specs(tb, 1)
    return pl.pallas_call(
        _peer_v_kernel, grid=(T // tb,),
        in_specs=[idx, picks, table], out_specs=feat,
        out_shape=jax.ShapeDtypeStruct((T, D_MODEL), F32),
        scratch_shapes=scratch,
        compiler_params=params, name="peer_v",
    )(e_grp, w, tab_v)


def _final_kernel(x1_ref, f_ref, p_ref, wg_ref, wp_ref, g_ref, b_ref, y_ref):
    x1 = x1_ref[...]
    e = jax.nn.sigmoid(_dot(x1, wg_ref[...])) * _dot(p_ref[...], wp_ref[...])
    y_ref[...] = _layer_norm(ALPHA * x1 + f_ref[...] + e, g_ref[...], b_ref[...])


def _final(x1, f, f_row_offset, p, wg, wp, ln_g, ln_b, tm):
    T = p.shape[0]
    off = f_row_offset // tm
    full = lambda a: pl.BlockSpec(a.shape, lambda i: (0,) * a.ndim)
    shifted = pl.BlockSpec((tm, D_MODEL), lambda i: (i + off, 0))
    return pl.pallas_call(
        _final_kernel,
        grid=(T // tm,),
        in_specs=[pl.BlockSpec((tm, D_MODEL), lambda i: (i, 0)), shifted,
                  pl.BlockSpec((tm, PLE_DIM), lambda i: (i, 0)),
                  full(wg), full(wp), full(ln_g), full(ln_b)],
        out_specs=pl.BlockSpec((tm, D_MODEL), lambda i: (i, 0)),
        out_shape=jax.ShapeDtypeStruct((T, D_MODEL), F32),
        compiler_params=pltpu.CompilerParams(
            dimension_semantics=("arbitrary",), vmem_limit_bytes=48 * MIB),
        name="final",
    )(x1, f, p, wg, wp, ln_g, ln_b)


def _t5_bias_by_distance(dist, table):
    n = jnp.maximum(dist, 0)
    max_exact = N_BUCKETS // 2
    nf = jnp.maximum(n, 1).astype(F32)
    large = max_exact + jnp.floor(jnp.log(nf / max_exact) / math.log(MAX_DISTANCE / max_exact)
                                  * (N_BUCKETS - max_exact)).astype(I32)
    large = jnp.minimum(large, N_BUCKETS - 1)
    bucket = jnp.where(n < max_exact, n, large)
    hit = bucket[None, :, None] == jnp.arange(N_BUCKETS)
    return jnp.sum(jnp.where(hit, table.T[:, None, :], 0.0), axis=-1)


def _band_bias(table, n_query, n_key, key_offset):
    d_max = n_query - 1 + key_offset
    d_min = key_offset - (n_key - 1)
    by_dist = _t5_bias_by_distance(jnp.arange(d_max, d_min - 1, -1), table)
    rows = [by_dist[:, d_max - (q + key_offset):d_max - (q + key_offset) + n_key]
            for q in range(n_query)]
    return jnp.stack(rows, axis=1)


def kernel(x_prompt, x_sample, cache_k_win, cache_v_win, p_prompt, p_sample, rel_bias_table,
           w_in, attn_sinks, w_att_out, c_ln_g, c_ln_b, c_ws, c_bs, w_chunk_out, w_o, ln1_g,
           ln1_b, peer_wq, peer_subkeys, peer_u, peer_v, w_ple_gate, w_ple_proj, ln2_g, ln2_b):
    batch, seq, d = x_prompt.shape
    n_seq, s_new, _ = x_sample.shape
    w_buf = cache_k_win.shape[2]
    assert d == D_MODEL and w_in.shape[0] == DEPTH and seq % ATT_BLOCK == 0
    tp = batch * seq
    ts = n_seq * s_new
    t_all = tp + ts
    nb = seq // ATT_BLOCK
    seqs_per_step = 8
    assert n_seq % seqs_per_step == 0 and tp % (seqs_per_step * s_new) == 0
    assert tp % 256 == 0 and ts % 256 == 0 and tp % PEER_TOKENS_PER_STEP == 0
    assert tp % ROUTE_TILE == 0 and ts % ROUTE_TILE == 0

    row2 = lambda v: v.reshape(1, -1)
    w_in_b = w_in[0].astype(BF16)
    wa, wc, wo = w_att_out[0].astype(BF16), w_chunk_out[0].astype(BF16), w_o[0].astype(BF16)
    sinks = attn_sinks[0]
    cg, cb = row2(c_ln_g[0]), row2(c_ln_b[0])
    l1g, l1b = row2(ln1_g[0]), row2(ln1_b[0])
    l2g, l2b = row2(ln2_g[0]), row2(ln2_b[0])
    ws, bs = c_ws[0], c_bs[0]

    bias_p = _band_bias(rel_bias_table, ATT_BLOCK, 2 * ATT_BLOCK, ATT_BLOCK)
    bias_s = _band_bias(rel_bias_table, s_new, w_buf + s_new, w_buf)
    bias_s = bias_s.reshape(N_KV, REP * s_new, w_buf + s_new)
    bias_sc, bias_sn = bias_s[:, :, :w_buf], bias_s[:, :, w_buf:]

    bs_exp_p = jnp.repeat(bs.T, C_GDIM, axis=1)
    bs_exp_s = jnp.tile(jnp.repeat(bs[:, :s_new].T, C_GDIM, axis=1), (seqs_per_step, 1))
    tril = jnp.tril(jnp.ones((s_new, s_new), F32))
    eye = jnp.eye(seqs_per_step, dtype=F32)
    wbd = jnp.stack([jnp.kron(eye, ws[g, :s_new, :s_new] * tril) for g in range(C_GROUPS)])

    xp = x_prompt.reshape(tp, d)
    q, kv, cu, cvn, gate = _inproj(xp, w_in_b, cg, cb, 256)
    x1p = _prompt_blocks(nb, sinks, q, kv, cu, cvn, gate, xp, bias_p, ws, bs_exp_p,
                        wa, wc, wo, l1g, l1b)
    kv_tail = kv.reshape(batch, seq, 2 * KV_WIDTH)[:, seq - w_buf:]
    kp = kv_tail[..., :KV_WIDTH].reshape(1, batch, w_buf, N_KV, HEAD_DIM)
    vp = kv_tail[..., KV_WIDTH:].reshape(1, batch, w_buf, N_KV, HEAD_DIM)

    xs = x_sample.reshape(ts, d)
    q, kv, cu, cvn_s, gate = _inproj(xs, w_in_b, cg, cb, 256)
    x1s, k_new, v_new = _sample_blocks(
        seqs_per_step, s_new, sinks, q, kv, cu, cvn_s, gate, xs,
        cache_k_win[0].reshape(n_seq, w_buf, KV_WIDTH),
        cache_v_win[0].reshape(n_seq, w_buf, KV_WIDTH),
        bias_sc, bias_sn, wbd, bs_exp_s, wa, wc, wo, l1g, l1b)
    ks_out = k_new.reshape(1, n_seq, w_buf, N_KV, HEAD_DIM)
    vs_out = v_new.reshape(1, n_seq, w_buf, N_KV, HEAD_DIM)
    cs_out = cvn_s.reshape(1, n_seq, s_new, C_WIDTH)

    e_t, g_t = _route(x1p, x1s, peer_wq[0].astype(BF16), peer_subkeys[0], ROUTE_TILE)
    nt = t_all // ROUTE_TILE
    e_grp = (e_t.reshape(nt, PEER_PICKS, ROUTE_TILE // PEER_GROUP, PEER_GROUP)
             .transpose(0, 2, 1, 3).reshape(t_all // PEER_GROUP, PEER_ROWS))
    g_tok = g_t.transpose(0, 2, 1).reshape(t_all, PEER_PICKS)
    tab_u = _pack_table(peer_u[0], 1024)
    tab_v = _pack_table(peer_v[0], 1024)
    w_tok = _peer_u(e_grp, x1p, x1s, g_tok, tab_u, PEER_TOKENS_PER_STEP)
    f = _peer_v(e_grp, w_tok, tab_v, PEER_TOKENS_PER_STEP)

    wg, wp = w_ple_gate[0].astype(BF16), w_ple_proj[0].astype(BF16)
    yp = _final(x1p, f, 0, p_prompt[0].reshape(tp, PLE_DIM), wg, wp, l2g, l2b,
                512 if tp % 512 == 0 else 256)
    ys = _final(x1s, f, tp, p_sample[0].reshape(ts, PLE_DIM), wg, wp, l2g, l2b, 256)
    return (yp.reshape(batch, seq, d), ys.reshape(n_seq, s_new, d), kp, vp, ks_out, vs_out,
            cs_out)
```

```python
import functools
import math

import jax
import jax.numpy as jnp
from jax import lax
from jax.experimental import pallas as pl
from jax.experimental.pallas import tpu as pltpu

F32 = jnp.float32
BF16 = jnp.bfloat16
I32 = jnp.int32

D_MODEL = 1024
N_HEADS = 8
N_KV = 2
REP = N_HEADS // N_KV
HEAD_DIM = 64
Q_WIDTH = N_HEADS * HEAD_DIM
KV_WIDTH = N_KV * HEAD_DIM
WINDOW = 128
ATT_BLOCK = 128
N_BUCKETS = 32
MAX_DISTANCE = 128
C_GROUPS = 4
C_CHUNK = 128
C_WIDTH = 512
C_GDIM = C_WIDTH // C_GROUPS
PLE_DIM = 256
PEER_HEADS = 8
N_KEYS = 128
N_EXPERTS = N_KEYS * N_KEYS
PEER_TOPK = 16
PEER_DKEY = 256
PEER_DHALF = PEER_DKEY // 2
PEER_PICKS = PEER_HEADS * PEER_TOPK
DEPTH = 1
ALPHA = (2.0 * DEPTH) ** 0.25
LN_EPS = 1e-5
NEG = -1e30
IN_WIDTH = Q_WIDTH + 2 * KV_WIDTH + 2 * C_WIDTH + 2 * D_MODEL

LANES = 128
SUBLANES = 8
MIB = 1024 * 1024

PEER_GROUP = SUBLANES
PEER_ROWS = PEER_GROUP * PEER_PICKS
PACK_CHUNKS = D_MODEL // (2 * LANES)
PEER_TOKENS_PER_STEP = 256
ROUTE_TILE = 512


def _layer_norm(x, g, b):
    mu = jnp.mean(x, axis=-1, keepdims=True)
    xc = x - mu
    var = jnp.mean(xc * xc, axis=-1, keepdims=True)
    return xc * lax.rsqrt(var + LN_EPS) * g + b


def _dot(a, b):
    return jnp.dot(a.astype(BF16), b.astype(BF16), preferred_element_type=F32)


def _dot_nt(a, b):
    return lax.dot_general(a.astype(BF16), b.astype(BF16), (((1,), (1,)), ((), ())),
                           preferred_element_type=F32)


def _inproj_kernel(x_ref, w_ref, g_ref, b_ref, q_ref, kv_ref, cu_ref, cvn_ref, gate_ref):
    z = jnp.dot(x_ref[...].astype(BF16), w_ref[...], preferred_element_type=F32)
    o = 0
    q_ref[...] = z[:, o:o + Q_WIDTH]
    o += Q_WIDTH
    kv_ref[...] = z[:, o:o + 2 * KV_WIDTH]
    o += 2 * KV_WIDTH
    cu_ref[...] = z[:, o:o + C_WIDTH]
    o += C_WIDTH
    cvn_ref[...] = _layer_norm(z[:, o:o + C_WIDTH], g_ref[...], b_ref[...])
    o += C_WIDTH
    gate_ref[...] = z[:, o:o + 2 * D_MODEL]


def _inproj(x, w_bf16, c_g, c_b, tm):
    T = x.shape[0]
    widths = (Q_WIDTH, 2 * KV_WIDTH, C_WIDTH, C_WIDTH, 2 * D_MODEL)
    row = lambda n: pl.BlockSpec((tm, n), lambda i: (i, 0))
    full = lambda a: pl.BlockSpec(a.shape, lambda i: (0,) * a.ndim)
    return pl.pallas_call(
        _inproj_kernel,
        grid=(T // tm,),
        in_specs=[row(D_MODEL), full(w_bf16), full(c_g), full(c_b)],
        out_specs=[row(n) for n in widths],
        out_shape=[jax.ShapeDtypeStruct((T, n), F32) for n in widths],
        compiler_params=pltpu.CompilerParams(
            dimension_semantics=("arbitrary",), vmem_limit_bytes=48 * MIB),
        name="inproj",
    )(x, w_bf16, c_g, c_b)


def _merge_and_norm(x, ya, yc, gate, wa_ref, wc_ref, wo_ref, g_ref, b_ref):
    ga = gate[:, :D_MODEL]
    gc = gate[:, D_MODEL:]
    mix = jax.nn.sigmoid(ga) * _dot(ya, wa_ref[...]) + jax.nn.sigmoid(gc) * _dot(yc, wc_ref[...])
    h = ALPHA * x + _dot(mix, wo_ref[...])
    return _layer_norm(h, g_ref[...], b_ref[...])


def _softmax_with_sink(s, sink):
    m = jnp.maximum(jnp.max(s, axis=-1, keepdims=True), sink)
    e = jnp.exp(s - m)
    return e / (jnp.sum(e, axis=-1, keepdims=True) + jnp.exp(sink - m))


PROMPT_BLOCKS_PER_STEP = 2


def _prompt_block_kernel(nb, sinks_ref, q_ref, kvp_ref, kvo_ref, cu_ref, cvn_ref, gate_ref,
                         x_ref, bias_ref, ws_ref, bs_ref, wa_ref, wc_ref, wo_ref, g_ref,
                         b_ref, x1_ref):
    blk = ATT_BLOCK
    qi = lax.broadcasted_iota(I32, (blk, 2 * blk), 0)
    kj = lax.broadcasted_iota(I32, (blk, 2 * blk), 1)
    dist = qi - kj + blk
    in_window = (dist >= 0) & (dist < WINDOW)
    ti = lax.broadcasted_iota(I32, (C_CHUNK, C_CHUNK), 0)
    si = lax.broadcasted_iota(I32, (C_CHUNK, C_CHUNK), 1)
    wms = [jnp.where(si <= ti, ws_ref[g], 0.0) for g in range(C_GROUPS)]

    scale = HEAD_DIM ** -0.5
    work = [(sb, h) for sb in range(PROMPT_BLOCKS_PER_STEP) for h in range(N_HEADS)]
    kvs, scores = {}, {}
    for sb in range(PROMPT_BLOCKS_PER_STEP):
        rows = slice(sb * blk, (sb + 1) * blk)
        kv_prev = kvp_ref[...] if sb == 0 else kvo_ref[(sb - 1) * blk:sb * blk, :]
        kvs[sb] = jnp.concatenate([kv_prev, kvo_ref[rows, :]], axis=0)
    for sb, h in work:
        n = lax.rem(pl.program_id(0) * PROMPT_BLOCKS_PER_STEP + sb, nb)
        mask = in_window & ((kj >= blk) | (n > 0))
        g = h // REP
        qh = q_ref[sb * blk:(sb + 1) * blk, h * HEAD_DIM:(h + 1) * HEAD_DIM]
        kg = kvs[sb][:, g * HEAD_DIM:(g + 1) * HEAD_DIM]
        scores[sb, h] = jnp.where(mask, _dot_nt(qh, kg) * scale + bias_ref[h], NEG)
    probs = {key: _softmax_with_sink(scores[key], sinks_ref[key[1]]) for key in work}
    ya_blocks, yc_blocks = [], []
    for sb in range(PROMPT_BLOCKS_PER_STEP):
        rows = slice(sb * blk, (sb + 1) * blk)
        outs = []
        for h in range(N_HEADS):
            g = h // REP
            vg = kvs[sb][:, KV_WIDTH + g * HEAD_DIM:KV_WIDTH + (g + 1) * HEAD_DIM]
            outs.append(_dot(probs[sb, h], vg))
        ya_blocks.append(jnp.concatenate(outs, axis=1))

        cvn = cvn_ref[rows, :]
        parts = [_dot(wms[g], cvn[:, g * C_GDIM:(g + 1) * C_GDIM]) for g in range(C_GROUPS)]
        yc_blocks.append(cu_ref[rows, :] * (jnp.concatenate(parts, axis=1) + bs_ref[...]))

    ya = jnp.concatenate(ya_blocks, axis=0)
    yc = jnp.concatenate(yc_blocks, axis=0)
    x1_ref[...] = _merge_and_norm(x_ref[...], ya, yc, gate_ref[...], wa_ref, wc_ref, wo_ref,
                                  g_ref, b_ref)


def _prompt_blocks(nb, sinks, q, kv, cu, cvn, gate, x, bias, ws, bs_exp,
                   wa, wc, wo, ln_g, ln_b):
    T = q.shape[0]
    blk = ATT_BLOCK
    tm = PROMPT_BLOCKS_PER_STEP * blk
    assert nb % PROMPT_BLOCKS_PER_STEP == 0
    row = lambda n: pl.BlockSpec((tm, n), lambda i: (i, 0))
    full = lambda a: pl.BlockSpec(a.shape, lambda i: (0,) * a.ndim)
    prev = pl.BlockSpec((blk, 2 * KV_WIDTH),
                        lambda i: (jnp.maximum(i * PROMPT_BLOCKS_PER_STEP - 1, 0), 0))
    return pl.pallas_call(
        functools.partial(_prompt_block_kernel, nb),
        grid=(T // tm,),
        in_specs=[pl.BlockSpec(memory_space=pltpu.SMEM),
                  row(Q_WIDTH), prev, row(2 * KV_WIDTH), row(C_WIDTH), row(C_WIDTH),
                  row(2 * D_MODEL), row(D_MODEL), full(bias), full(ws), full(bs_exp),
                  full(wa), full(wc), full(wo), full(ln_g), full(ln_b)],
        out_specs=row(D_MODEL),
        out_shape=jax.ShapeDtypeStruct((T, D_MODEL), F32),
        compiler_params=pltpu.CompilerParams(
            dimension_semantics=("arbitrary",), vmem_limit_bytes=48 * MIB),
        name="prompt_blocks",
    )(sinks, q, kv, kv, cu, cvn, gate, x, bias, ws, bs_exp, wa, wc, wo, ln_g, ln_b)


def _sample_block_kernel(seqs, s_new, sinks_ref, q_ref, kv_ref, cu_ref, cvn_ref, gate_ref,
                         x_ref, ck_ref, cv_ref, biasc_ref, biasn_ref, wbd_ref, bs_ref, wa_ref,
                         wc_ref, wo_ref, g_ref, b_ref, x1_ref, kout_ref, vout_ref):
    w_buf = ck_ref.shape[1]
    rows = REP * s_new
    qi_c = lax.rem(lax.broadcasted_iota(I32, (rows, w_buf), 0), s_new)
    kj_c = lax.broadcasted_iota(I32, (rows, w_buf), 1)
    mask_c = (qi_c + w_buf - kj_c) < WINDOW
    qi_n = lax.rem(lax.broadcasted_iota(I32, (rows, s_new), 0), s_new)
    kj_n = lax.broadcasted_iota(I32, (rows, s_new), 1)
    mask_n = kj_n <= qi_n
    sink_col = [jnp.concatenate([jnp.full((s_new, 1), sinks_ref[g * REP + r], F32)
                                 for r in range(REP)], axis=0) for g in range(N_KV)]

    q_all = q_ref[...]
    kv_all = kv_ref[...]
    scale = HEAD_DIM ** -0.5
    work = [(b, g) for b in range(seqs) for g in range(N_KV)]
    ck, cv, kvb, scores = {}, {}, {}, {}
    for b in range(seqs):
        kvb[b] = kv_all[b * s_new:(b + 1) * s_new]
        ck[b] = ck_ref[b]
        cv[b] = cv_ref[b]
        kout_ref[b] = jnp.concatenate([ck[b][s_new:], kvb[b][:, :KV_WIDTH]], axis=0)
        vout_ref[b] = jnp.concatenate([cv[b][s_new:], kvb[b][:, KV_WIDTH:]], axis=0)
    for b, g in work:
        qb = q_all[b * s_new:(b + 1) * s_new]
        qg = jnp.concatenate([qb[:, (g * REP + r) * HEAD_DIM:(g * REP + r + 1) * HEAD_DIM]
                              for r in range(REP)], axis=0)
        lane = slice(g * HEAD_DIM, (g + 1) * HEAD_DIM)
        sc = _dot_nt(qg, ck[b][:, lane]) * scale + biasc_ref[g]
        sn = _dot_nt(qg, kvb[b][:, lane]) * scale + biasn_ref[g]
        scores[b, g] = (jnp.where(mask_c, sc, NEG), jnp.where(mask_n, sn, NEG))
    probs = {}
    for b, g in work:
        sc, sn = scores[b, g]
        sink = sink_col[g]
        m = jnp.maximum(jnp.maximum(jnp.max(sc, axis=-1, keepdims=True),
                                    jnp.max(sn, axis=-1, keepdims=True)), sink)
        ec = jnp.exp(sc - m)
        en = jnp.exp(sn - m)
        den = (jnp.sum(ec, axis=-1, keepdims=True) + jnp.sum(en, axis=-1, keepdims=True)
               + jnp.exp(sink - m))
        probs[b, g] = (ec / den, en / den)
    ya_rows = []
    for b in range(seqs):
        heads = []
        for g in range(N_KV):
            lane = slice(g * HEAD_DIM, (g + 1) * HEAD_DIM)
            vlane = slice(KV_WIDTH + g * HEAD_DIM, KV_WIDTH + (g + 1) * HEAD_DIM)
            pc, pn = probs[b, g]
            og = _dot(pc, cv[b][:, lane]) + _dot(pn, kvb[b][:, vlane])
            heads.extend(og[r * s_new:(r + 1) * s_new] for r in range(REP))
        ya_rows.append(jnp.concatenate(heads, axis=1))
    ya = jnp.concatenate(ya_rows, axis=0)

    cvn = cvn_ref[...]
    parts = [_dot(wbd_ref[g], cvn[:, g * C_GDIM:(g + 1) * C_GDIM]) for g in range(C_GROUPS)]
    yc = cu_ref[...] * (jnp.concatenate(parts, axis=1) + bs_ref[...])

    x1_ref[...] = _merge_and_norm(x_ref[...], ya, yc, gate_ref[...], wa_ref, wc_ref, wo_ref,
                                  g_ref, b_ref)


def _sample_blocks(seqs, s_new, sinks, q, kv, cu, cvn, gate, x, cache_k,
                   cache_v, bias_c, bias_n, wbd, bs_exp, wa, wc, wo, ln_g, ln_b):
    T = q.shape[0]
    n_seq, w_buf, kvw = cache_k.shape
    tm = seqs * s_new
    row = lambda n: pl.BlockSpec((tm, n), lambda i: (i, 0))
    full = lambda a: pl.BlockSpec(a.shape, lambda i: (0,) * a.ndim)
    cache = pl.BlockSpec((seqs, w_buf, kvw), lambda i: (i, 0, 0))
    return pl.pallas_call(
        functools.partial(_sample_block_kernel, seqs, s_new),
        grid=(T // tm,),
        in_specs=[pl.BlockSpec(memory_space=pltpu.SMEM),
                  row(Q_WIDTH), row(2 * KV_WIDTH), row(C_WIDTH), row(C_WIDTH),
                  row(2 * D_MODEL), row(D_MODEL), cache, cache, full(bias_c), full(bias_n),
                  full(wbd), full(bs_exp), full(wa), full(wc), full(wo), full(ln_g),
                  full(ln_b)],
        out_specs=[row(D_MODEL), cache, cache],
        out_shape=[jax.ShapeDtypeStruct((T, D_MODEL), F32),
                   jax.ShapeDtypeStruct(cache_k.shape, F32),
                   jax.ShapeDtypeStruct(cache_v.shape, F32)],
        compiler_params=pltpu.CompilerParams(
            dimension_semantics=("arbitrary",), vmem_limit_bytes=48 * MIB),
        name="sample_blocks",
    )(sinks, q, kv, cu, cvn, gate, x, cache_k, cache_v, bias_c, bias_n, wbd, bs_exp, wa, wc,
      wo, ln_g, ln_b)


def _two_part_specs(tiles_a, tile_rows, width):
    first = pl.BlockSpec((tile_rows, width), lambda i: (jnp.minimum(i, tiles_a - 1), 0))
    second = pl.BlockSpec((tile_rows, width), lambda i: (jnp.maximum(i - tiles_a, 0), 0))
    return first, second


def _two_part_tile(tiles_a, a_ref, b_ref):
    return jnp.where(pl.program_id(0) < tiles_a, a_ref[...], b_ref[...])


def _top16(s, iota, fill):
    vals, idxs = [], []
    for _ in range(PEER_TOPK):
        m = jnp.max(s, axis=0, keepdims=True)
        i = jnp.min(jnp.where(s == m, iota, fill), axis=0, keepdims=True)
        vals.append(m)
        idxs.append(i)
        s = jnp.where(iota == i, -jnp.inf, s)
    return jnp.concatenate(vals, axis=0), jnp.concatenate(idxs, axis=0)


def _pick(table, iota, idx):
    return jnp.concatenate(
        [jnp.sum(jnp.where(iota == idx[r:r + 1], table, 0), axis=0, keepdims=True)
         for r in range(PEER_TOPK)], axis=0)


_CAND_SHORT = SUBLANES


def _cand_flat_index(tt):
    i16 = lax.broadcasted_iota(I32, (PEER_TOPK, tt), 0)
    i8 = lax.broadcasted_iota(I32, (_CAND_SHORT, tt), 0)
    pieces = [i16] + [k * PEER_TOPK + i8 for k in range(1, _CAND_SHORT)]
    pieces.append((_CAND_SHORT + i8) * PEER_TOPK)
    return jnp.concatenate(pieces, axis=0)


def _cand_values(sv0, sv1):
    pieces = [sv0[0:1] + sv1]
    pieces += [sv0[k:k + 1] + sv1[0:_CAND_SHORT] for k in range(1, _CAND_SHORT)]
    pieces.append(sv0[_CAND_SHORT:] + sv1[0:1])
    return jnp.concatenate(pieces, axis=0)


def _route_kernel(tiles_a, xa_ref, xb_ref, wq_ref, sk_ref, e_ref, g_ref, q_scr, gt_scr):
    tt = xa_ref.shape[0]
    x1 = _two_part_tile(tiles_a, xa_ref, xb_ref)
    qf = jnp.dot(x1.astype(BF16), wq_ref[...], preferred_element_type=F32)
    for hc in range(2 * PEER_HEADS):
        q_scr[hc] = qf[:, hc * PEER_DHALF:(hc + 1) * PEER_DHALF]
    iota_n = lax.broadcasted_iota(I32, (N_KEYS, tt), 0)
    iota_k = lax.broadcasted_iota(I32, (PEER_TOPK, tt), 0)
    flat = _cand_flat_index(tt)

    def first_stage(h):
        out = []
        for c in range(2):
            s = _dot_nt(sk_ref[h, c], q_scr[2 * h + c])
            out.extend(_top16(s, iota_n, N_KEYS))
        return out[0], out[2], out[1], out[3]

    def second_stage(h, carry):
        sv0, sv1, si0, si1 = carry
        fv, fi = _top16(_cand_values(sv0, sv1), flat, PEER_TOPK * PEER_TOPK)
        i1 = _pick(si0, iota_k, lax.shift_right_logical(fi, 4))
        i2 = _pick(si1, iota_k, lax.bitwise_and(fi, PEER_TOPK - 1))
        ex = jnp.exp(fv - fv[0:1])
        row0 = pl.multiple_of(h * PEER_TOPK, PEER_TOPK)
        e_ref[0, pl.ds(row0, PEER_TOPK), :] = (i1 * N_KEYS + i2) * PACK_CHUNKS
        gt_scr[pl.ds(row0, PEER_TOPK), :] = ex / jnp.sum(ex, axis=0, keepdims=True)

    def body(h, carry):
        second_stage(h - 1, carry)
        return first_stage(h)

    last = lax.fori_loop(1, PEER_HEADS, body, first_stage(0))
    second_stage(PEER_HEADS - 1, last)
    g_ref[...] = gt_scr[...].T


def _route(x1a, x1b, wq_bf16, subkeys, tt):
    T = x1a.shape[0] + x1b.shape[0]
    tiles_a = x1a.shape[0] // tt
    full = lambda a: pl.BlockSpec(a.shape, lambda i: (0,) * a.ndim)
    out = pl.BlockSpec((1, PEER_PICKS, tt), lambda i: (i, 0, 0))
    return pl.pallas_call(
        functools.partial(_route_kernel, tiles_a),
        grid=(T // tt,),
        in_specs=[*_two_part_specs(tiles_a, tt, D_MODEL), full(wq_bf16), full(subkeys)],
        out_specs=[out, pl.BlockSpec((tt, PEER_PICKS), lambda i: (i, 0))],
        out_shape=[jax.ShapeDtypeStruct((T // tt, PEER_PICKS, tt), I32),
                   jax.ShapeDtypeStruct((T, PEER_PICKS), F32)],
        scratch_shapes=[pltpu.VMEM((2 * PEER_HEADS, tt, PEER_DHALF), F32),
                        pltpu.VMEM((PEER_PICKS, tt), F32)],
        compiler_params=pltpu.CompilerParams(
            dimension_semantics=("arbitrary",), vmem_limit_bytes=48 * MIB),
        name="peer_route",
    )(x1a, x1b, wq_bf16, subkeys)


def _pack_kernel(t_ref, o_ref):
    half = D_MODEL // 2
    o_ref[...] = pltpu.pack_elementwise([t_ref[:, :half], t_ref[:, half:]], packed_dtype=BF16)


def _pack_table(tab, tm):
    n = tab.shape[0]
    packed = pl.pallas_call(
        _pack_kernel,
        grid=(n // tm,),
        in_specs=[pl.BlockSpec((tm, D_MODEL), lambda i: (i, 0))],
        out_specs=pl.BlockSpec((tm, D_MODEL // 2), lambda i: (i, 0)),
        out_shape=jax.ShapeDtypeStruct((n, D_MODEL // 2), jnp.uint32),
        compiler_params=pltpu.CompilerParams(dimension_semantics=("arbitrary",)),
        name="pack_table",
    )(tab)
    return packed.reshape(n * PACK_CHUNKS, LANES)


def _unpack(word):
    return (pltpu.unpack_elementwise(word, index=0, packed_dtype=BF16, unpacked_dtype=F32),
            pltpu.unpack_elementwise(word, index=1, packed_dtype=BF16, unpacked_dtype=F32))


def _gather_group(idx_ref, tab_ref, tile_ref):
    for r in range(PEER_ROWS):
        row = pl.multiple_of(idx_ref[r], PACK_CHUNKS)
        tile_ref[pl.ds(r * PACK_CHUNKS, PACK_CHUNKS), :] = tab_ref[pl.ds(row, PACK_CHUNKS), :]


def _tile_rows(tile_ref, c, j):
    start = j * PEER_GROUP * PACK_CHUNKS + c
    return tile_ref[pl.ds(start, PEER_GROUP, stride=PACK_CHUNKS), :]


def _index_copy(e_hbm, group, buf, sem, slot):
    return pltpu.make_async_copy(e_hbm.at[group], buf, sem.at[slot])


def _for_each_group(e_hbm, idx_bufs, sem, groups_per_step, process, groups_per_block=1):
    per_iter = 2 * groups_per_block
    assert len(idx_bufs) == per_iter and groups_per_step % per_iter == 0
    step = pl.program_id(0)
    total = pl.num_programs(0) * groups_per_step
    base = step * groups_per_step

    @pl.when(step == 0)
    def _():
        for slot in range(per_iter):
            _index_copy(e_hbm, slot, idx_bufs[slot], sem, slot).start()

    def body(it, carry):
        first = base + per_iter * it
        for b in range(2):
            slots = range(b * groups_per_block, (b + 1) * groups_per_block)
            for slot in slots:
                _index_copy(e_hbm, first + slot, idx_bufs[slot], sem, slot).wait()
            for k, slot in enumerate(slots):
                process(idx_bufs[slot], per_iter * it + slot, k)

            @pl.when(first + per_iter + slots[0] < total)
            def _():
                for slot in slots:
                    _index_copy(e_hbm, first + per_iter + slot, idx_bufs[slot], sem, slot).start()
        return carry

    lax.fori_loop(0, groups_per_step // per_iter, body, 0)


U_GROUPS_PER_BLOCK = 4


def _peer_u_kernel(tiles_a, e_hbm, xa_ref, xb_ref, g_ref, tab_ref, w_ref, x_ref, *scratch):
    tiles = scratch[:U_GROUPS_PER_BLOCK]
    idx_bufs, sem = scratch[U_GROUPS_PER_BLOCK:-1], scratch[-1]
    lane = lax.broadcasted_iota(I32, (PEER_GROUP, PEER_PICKS), 1)
    half = D_MODEL // 2
    x_ref[...] = _two_part_tile(tiles_a, xa_ref, xb_ref)

    def process(idx_ref, local, k):
        tile_ref = tiles[k]
        t0 = pl.multiple_of(local * PEER_GROUP, PEER_GROUP)
        _gather_group(idx_ref, tab_ref, tile_ref)
        xg = x_ref[pl.ds(t0, PEER_GROUP), :]
        x_lo = [xg[:, c * LANES:(c + 1) * LANES] for c in range(PACK_CHUNKS)]
        x_hi = [xg[:, half + c * LANES:half + (c + 1) * LANES] for c in range(PACK_CHUNKS)]
        a = jnp.zeros((PEER_GROUP, PEER_PICKS), F32)
        for j in range(PEER_PICKS):
            p = None
            for c in range(PACK_CHUNKS):
                lo, hi = _unpack(_tile_rows(tile_ref, c, j))
                term = lo * x_lo[c] + hi * x_hi[c]
                p = term if p is None else p + term
            a = jnp.where(lane == j, jnp.sum(p, axis=1, keepdims=True), a)
        w_ref[pl.ds(t0, PEER_GROUP), :] = g_ref[pl.ds(t0, PEER_GROUP), :] * jax.nn.gelu(a)

    _for_each_group(e_hbm, idx_bufs, sem, x_ref.shape[0] // PEER_GROUP, process,
                    U_GROUPS_PER_BLOCK)


def _peer_v_kernel(e_hbm, w_ref, tab_ref, f_ref, tile_ref, idx_a, idx_b, sem):
    def process(idx_ref, local, k):
        t0 = pl.multiple_of(local * PEER_GROUP, PEER_GROUP)
        _gather_group(idx_ref, tab_ref, tile_ref)
        wg = w_ref[pl.ds(t0, PEER_GROUP), :]
        acc = [jnp.zeros((PEER_GROUP, LANES), F32) for _ in range(2 * PACK_CHUNKS)]
        for j in range(PEER_PICKS):
            wb = jnp.broadcast_to(wg[:, j:j + 1], (PEER_GROUP, LANES))
            for c in range(PACK_CHUNKS):
                lo, hi = _unpack(_tile_rows(tile_ref, c, j))
                acc[c] = acc[c] + wb * lo
                acc[PACK_CHUNKS + c] = acc[PACK_CHUNKS + c] + wb * hi
        f_ref[pl.ds(t0, PEER_GROUP), :] = jnp.concatenate(acc, axis=1)

    _for_each_group(e_hbm, (idx_a, idx_b), sem, w_ref.shape[0] // PEER_GROUP, process)


def _peer_specs(tb, groups_per_block):
    idx = pl.BlockSpec(memory_space=pl.ANY)
    picks = pl.BlockSpec((tb, PEER_PICKS), lambda i: (i, 0))
    feat = pl.BlockSpec((tb, D_MODEL), lambda i: (i, 0))
    table = pl.BlockSpec((N_EXPERTS * PACK_CHUNKS, LANES), lambda i: (0, 0),
                         pipeline_mode=pl.Buffered(1))
    n_idx = 2 * groups_per_block
    scratch = ([pltpu.VMEM((PACK_CHUNKS * PEER_ROWS, LANES), jnp.uint32)] * groups_per_block
               + [pltpu.SMEM((PEER_ROWS,), I32)] * n_idx + [pltpu.SemaphoreType.DMA((n_idx,))])
    params = pltpu.CompilerParams(dimension_semantics=("arbitrary",),
                                  vmem_limit_bytes=48 * MIB)
    return idx, picks, feat, table, scratch, params


def _peer_u(e_grp, x1a, x1b, g, tab_u, tb):
    T = x1a.shape[0] + x1b.shape[0]
    tiles_a = x1a.shape[0] // tb
    idx, picks, feat, table, scratch, params = _peer_specs(tb, U_GROUPS_PER_BLOCK)
    return pl.pallas_call(
        functools.partial(_peer_u_kernel, tiles_a), grid=(T // tb,),
        in_specs=[idx, *_two_part_specs(tiles_a, tb, D_MODEL), picks, table], out_specs=picks,
        out_shape=jax.ShapeDtypeStruct((T, PEER_PICKS), F32),
        scratch_shapes=[pltpu.VMEM((tb, D_MODEL), F32)] + scratch,
        compiler_params=params,
        name="peer_u",
    )(e_grp, x1a, x1b, g, tab_u)


def _peer_v(e_grp, w, tab_v, tb):
    T = w.shape[0]
    idx, picks, feat, table, scratch, params = _peer_specs(tb, 1)
    return pl.pallas_call(
        _peer_v_kernel, grid=(T // tb,),
        in_specs=[idx, picks, table], out_specs=feat,
        out_shape=jax.ShapeDtypeStruct((T, D_MODEL), F32),
        scratch_shapes=scratch,
        compiler_params=params, name="peer_v",
    )(e_grp, w, tab_v)


def _final_kernel(x1_ref, f_ref, p_ref, wg_ref, wp_ref, g_ref, b_ref, y_ref):
    x1 = x1_ref[...]
    e = jax.nn.sigmoid(_dot(x1, wg_ref[...])) * _dot(p_ref[...], wp_ref[...])
    y_ref[...] = _layer_norm(ALPHA * x1 + f_ref[...] + e, g_ref[...], b_ref[...])


def _final(x1, f, f_row_offset, p, wg, wp, ln_g, ln_b, tm):
    T = p.shape[0]
    off = f_row_offset // tm
    full = lambda a: pl.BlockSpec(a.shape, lambda i: (0,) * a.ndim)
    shifted = pl.BlockSpec((tm, D_MODEL), lambda i: (i + off, 0))
    return pl.pallas_call(
        _final_kernel,
        grid=(T // tm,),
        in_specs=[pl.BlockSpec((tm, D_MODEL), lambda i: (i, 0)), shifted,
                  pl.BlockSpec((tm, PLE_DIM), lambda i: (i, 0)),
                  full(wg), full(wp), full(ln_g), full(ln_b)],
        out_specs=pl.BlockSpec((tm, D_MODEL), lambda i: (i, 0)),
        out_shape=jax.ShapeDtypeStruct((T, D_MODEL), F32),
        compiler_params=pltpu.CompilerParams(
            dimension_semantics=("arbitrary",), vmem_limit_bytes=48 * MIB),
        name="final",
    )(x1, f, p, wg, wp, ln_g, ln_b)


def _t5_bias_by_distance(dist, table):
    n = jnp.maximum(dist, 0)
    max_exact = N_BUCKETS // 2
    nf = jnp.maximum(n, 1).astype(F32)
    large = max_exact + jnp.floor(jnp.log(nf / max_exact) / math.log(MAX_DISTANCE / max_exact)
                                  * (N_BUCKETS - max_exact)).astype(I32)
    large = jnp.minimum(large, N_BUCKETS - 1)
    bucket = jnp.where(n < max_exact, n, large)
    hit = bucket[None, :, None] == jnp.arange(N_BUCKETS)
    return jnp.sum(jnp.where(hit, table.T[:, None, :], 0.0), axis=-1)


def _band_bias(table, n_query, n_key, key_offset):
    d_max = n_query - 1 + key_offset
    d_min = key_offset - (n_key - 1)
    by_dist = _t5_bias_by_distance(jnp.arange(d_max, d_min - 1, -1), table)
    rows = [by_dist[:, d_max - (q + key_offset):d_max - (q + key_offset) + n_key]
            for q in range(n_query)]
    return jnp.stack(rows, axis=1)


def kernel(x_prompt, x_sample, cache_k_win, cache_v_win, p_prompt, p_sample, rel_bias_table,
           w_in, attn_sinks, w_att_out, c_ln_g, c_ln_b, c_ws, c_bs, w_chunk_out, w_o, ln1_g,
           ln1_b, peer_wq, peer_subkeys, peer_u, peer_v, w_ple_gate, w_ple_proj, ln2_g, ln2_b):
    batch, seq, d = x_prompt.shape
    n_seq, s_new, _ = x_sample.shape
    w_buf = cache_k_win.shape[2]
    assert d == D_MODEL and w_in.shape[0] == DEPTH and seq % ATT_BLOCK == 0
    tp = batch * seq
    ts = n_seq * s_new
    t_all = tp + ts
    nb = seq // ATT_BLOCK
    seqs_per_step = 8
    assert n_seq % seqs_per_step == 0 and tp % (seqs_per_step * s_new) == 0
    assert tp % 256 == 0 and ts % 256 == 0 and tp % PEER_TOKENS_PER_STEP == 0
    assert tp % ROUTE_TILE == 0 and ts % ROUTE_TILE == 0

    row2 = lambda v: v.reshape(1, -1)
    w_in_b = w_in[0].astype(BF16)
    wa, wc, wo = w_att_out[0].astype(BF16), w_chunk_out[0].astype(BF16), w_o[0].astype(BF16)
    sinks = attn_sinks[0]
    cg, cb = row2(c_ln_g[0]), row2(c_ln_b[0])
    l1g, l1b = row2(ln1_g[0]), row2(ln1_b[0])
    l2g, l2b = row2(ln2_g[0]), row2(ln2_b[0])
    ws, bs = c_ws[0], c_bs[0]

    bias_p = _band_bias(rel_bias_table, ATT_BLOCK, 2 * ATT_BLOCK, ATT_BLOCK)
    bias_s = _band_bias(rel_bias_table, s_new, w_buf + s_new, w_buf)
    bias_s = bias_s.reshape(N_KV, REP * s_new, w_buf + s_new)
    bias_sc, bias_sn = bias_s[:, :, :w_buf], bias_s[:, :, w_buf:]

    bs_exp_p = jnp.repeat(bs.T, C_GDIM, axis=1)
    bs_exp_s = jnp.tile(jnp.repeat(bs[:, :s_new].T, C_GDIM, axis=1), (seqs_per_step, 1))
    tril = jnp.tril(jnp.ones((s_new, s_new), F32))
    eye = jnp.eye(seqs_per_step, dtype=F32)
    wbd = jnp.stack([jnp.kron(eye, ws[g, :s_new, :s_new] * tril) for g in range(C_GROUPS)])

    xp = x_prompt.reshape(tp, d)
    q, kv, cu, cvn, gate = _inproj(xp, w_in_b, cg, cb, 256)
    x1p = _prompt_blocks(nb, sinks, q, kv, cu, cvn, gate, xp, bias_p, ws, bs_exp_p,
                        wa, wc, wo, l1g, l1b)
    kv_tail = kv.reshape(batch, seq, 2 * KV_WIDTH)[:, seq - w_buf:]
    kp = kv_tail[..., :KV_WIDTH].reshape(1, batch, w_buf, N_KV, HEAD_DIM)
    vp = kv_tail[..., KV_WIDTH:].reshape(1, batch, w_buf, N_KV, HEAD_DIM)

    xs = x_sample.reshape(ts, d)
    q, kv, cu, cvn_s, gate = _inproj(xs, w_in_b, cg, cb, 256)
    x1s, k_new, v_new = _sample_blocks(
        seqs_per_step, s_new, sinks, q, kv, cu, cvn_s, gate, xs,
        cache_k_win[0].reshape(n_seq, w_buf, KV_WIDTH),
        cache_v_win[0].reshape(n_seq, w_buf, KV_WIDTH),
        bias_sc, bias_sn, wbd, bs_exp_s, wa, wc, wo, l1g, l1b)
    ks_out = k_new.reshape(1, n_seq, w_buf, N_KV, HEAD_DIM)
    vs_out = v_new.reshape(1, n_seq, w_buf, N_KV, HEAD_DIM)
    cs_out = cvn_s.reshape(1, n_seq, s_new, C_WIDTH)

    e_t, g_tok = _route(x1p, x1s, peer_wq[0].astype(BF16), peer_subkeys[0], ROUTE_TILE)
    nt = t_all // ROUTE_TILE
    e_grp = (e_t.reshape(nt, PEER_PICKS, ROUTE_TILE // PEER_GROUP, PEER_GROUP)
             .transpose(0, 2, 1, 3).reshape(t_all // PEER_GROUP, PEER_ROWS))
    tab_u = _pack_table(peer_u[0], 1024)
    tab_v = _pack_table(peer_v[0], 1024)
    w_tok = _peer_u(e_grp, x1p, x1s, g_tok, tab_u, PEER_TOKENS_PER_STEP)
    f = _peer_v(e_grp, w_tok, tab_v, PEER_TOKENS_PER_STEP)

    wg, wp = w_ple_gate[0].astype(BF16), w_ple_proj[0].astype(BF16)
    yp = _final(x1p, f, 0, p_prompt[0].reshape(tp, PLE_DIM), wg, wp, l2g, l2b,
                512 if tp % 512 == 0 else 256)
    ys = _final(x1s, f, tp, p_sample[0].reshape(ts, PLE_DIM), wg, wp, l2g, l2b, 256)
    return (yp.reshape(batch, seq, d), ys.reshape(n_seq, s_new, d), kp, vp, ks_out, vs_out,
            cs_out)
```

```python
import functools
import math

import jax
import jax.numpy as jnp
from jax import lax
from jax.experimental import pallas as pl
from jax.experimental.pallas import tpu as pltpu

F32 = jnp.float32
BF16 = jnp.bfloat16
I32 = jnp.int32

D_MODEL = 1024
N_HEADS = 8
N_KV = 2
REP = N_HEADS // N_KV
HEAD_DIM = 64
Q_WIDTH = N_HEADS * HEAD_DIM
KV_WIDTH = N_KV * HEAD_DIM
WINDOW = 128
ATT_BLOCK = 128
N_BUCKETS = 32
MAX_DISTANCE = 128
C_GROUPS = 4
C_CHUNK = 128
C_WIDTH = 512
C_GDIM = C_WIDTH // C_GROUPS
PLE_DIM = 256
PEER_HEADS = 8
N_KEYS = 128
N_EXPERTS = N_KEYS * N_KEYS
PEER_TOPK = 16
PEER_DKEY = 256
PEER_DHALF = PEER_DKEY // 2
PEER_PICKS = PEER_HEADS * PEER_TOPK
DEPTH = 1
ALPHA = (2.0 * DEPTH) ** 0.25
LN_EPS = 1e-5
NEG = -1e30
IN_WIDTH = Q_WIDTH + 2 * KV_WIDTH + 2 * C_WIDTH + 2 * D_MODEL

LANES = 128
SUBLANES = 8
MIB = 1024 * 1024

PEER_GROUP = SUBLANES
PEER_ROWS = PEER_GROUP * PEER_PICKS
PACK_CHUNKS = D_MODEL // (2 * LANES)
PEER_TOKENS_PER_STEP = 256
ROUTE_TILE = 512


def _layer_norm(x, g, b):
    mu = jnp.mean(x, axis=-1, keepdims=True)
    xc = x - mu
    var = jnp.mean(xc * xc, axis=-1, keepdims=True)
    return xc * lax.rsqrt(var + LN_EPS) * g + b


def _dot(a, b):
    return jnp.dot(a.astype(BF16), b.astype(BF16), preferred_element_type=F32)


def _dot_nt(a, b):
    return lax.dot_general(a.astype(BF16), b.astype(BF16), (((1,), (1,)), ((), ())),
                           preferred_element_type=F32)


def _inproj_kernel(x_ref, w_ref, g_ref, b_ref, q_ref, kv_ref, cu_ref, cvn_ref, gate_ref):
    z = jnp.dot(x_ref[...].astype(BF16), w_ref[...], preferred_element_type=F32)
    o = 0
    q_ref[...] = z[:, o:o + Q_WIDTH]
    o += Q_WIDTH
    kv_ref[...] = z[:, o:o + 2 * KV_WIDTH]
    o += 2 * KV_WIDTH
    cu_ref[...] = z[:, o:o + C_WIDTH]
    o += C_WIDTH
    cvn_ref[...] = _layer_norm(z[:, o:o + C_WIDTH], g_ref[...], b_ref[...])
    o += C_WIDTH
    gate_ref[...] = z[:, o:o + 2 * D_MODEL]


def _inproj(x, w_bf16, c_g, c_b, tm):
    T = x.shape[0]
    widths = (Q_WIDTH, 2 * KV_WIDTH, C_WIDTH, C_WIDTH, 2 * D_MODEL)
    row = lambda n: pl.BlockSpec((tm, n), lambda i: (i, 0))
    full = lambda a: pl.BlockSpec(a.shape, lambda i: (0,) * a.ndim)
    return pl.pallas_call(
        _inproj_kernel,
        grid=(T // tm,),
        in_specs=[row(D_MODEL), full(w_bf16), full(c_g), full(c_b)],
        out_specs=[row(n) for n in widths],
        out_shape=[jax.ShapeDtypeStruct((T, n), F32) for n in widths],
        compiler_params=pltpu.CompilerParams(
            dimension_semantics=("arbitrary",), vmem_limit_bytes=48 * MIB),
        name="inproj",
    )(x, w_bf16, c_g, c_b)


def _merge_and_norm(x, ya, yc, gate, wa_ref, wc_ref, wo_ref, g_ref, b_ref):
    ga = gate[:, :D_MODEL]
    gc = gate[:, D_MODEL:]
    mix = jax.nn.sigmoid(ga) * _dot(ya, wa_ref[...]) + jax.nn.sigmoid(gc) * _dot(yc, wc_ref[...])
    h = ALPHA * x + _dot(mix, wo_ref[...])
    return _layer_norm(h, g_ref[...], b_ref[...])


def _softmax_with_sink(s, sink):
    m = jnp.maximum(jnp.max(s, axis=-1, keepdims=True), sink)
    e = jnp.exp(s - m)
    return e / (jnp.sum(e, axis=-1, keepdims=True) + jnp.exp(sink - m))


PROMPT_BLOCKS_PER_STEP = 2


def _prompt_block_kernel(nb, sinks_ref, q_ref, kvp_ref, kvo_ref, cu_ref, cvn_ref, gate_ref,
                         x_ref, bias_ref, ws_ref, bs_ref, wa_ref, wc_ref, wo_ref, g_ref,
                         b_ref, x1_ref):
    blk = ATT_BLOCK
    qi = lax.broadcasted_iota(I32, (blk, 2 * blk), 0)
    kj = lax.broadcasted_iota(I32, (blk, 2 * blk), 1)
    dist = qi - kj + blk
    in_window = (dist >= 0) & (dist < WINDOW)
    ti = lax.broadcasted_iota(I32, (C_CHUNK, C_CHUNK), 0)
    si = lax.broadcasted_iota(I32, (C_CHUNK, C_CHUNK), 1)
    wms = [jnp.where(si <= ti, ws_ref[g], 0.0) for g in range(C_GROUPS)]

    scale = HEAD_DIM ** -0.5
    work = [(sb, h) for sb in range(PROMPT_BLOCKS_PER_STEP) for h in range(N_HEADS)]
    kvs, scores = {}, {}
    for sb in range(PROMPT_BLOCKS_PER_STEP):
        rows = slice(sb * blk, (sb + 1) * blk)
        kv_prev = kvp_ref[...] if sb == 0 else kvo_ref[(sb - 1) * blk:sb * blk, :]
        kvs[sb] = jnp.concatenate([kv_prev, kvo_ref[rows, :]], axis=0)
    for sb, h in work:
        n = lax.rem(pl.program_id(0) * PROMPT_BLOCKS_PER_STEP + sb, nb)
        mask = in_window & ((kj >= blk) | (n > 0))
        g = h // REP
        qh = q_ref[sb * blk:(sb + 1) * blk, h * HEAD_DIM:(h + 1) * HEAD_DIM]
        kg = kvs[sb][:, g * HEAD_DIM:(g + 1) * HEAD_DIM]
        scores[sb, h] = jnp.where(mask, _dot_nt(qh, kg) * scale + bias_ref[h], NEG)
    probs = {key: _softmax_with_sink(scores[key], sinks_ref[key[1]]) for key in work}
    ya_blocks, yc_blocks = [], []
    for sb in range(PROMPT_BLOCKS_PER_STEP):
        rows = slice(sb * blk, (sb + 1) * blk)
        outs = []
        for h in range(N_HEADS):
            g = h // REP
            vg = kvs[sb][:, KV_WIDTH + g * HEAD_DIM:KV_WIDTH + (g + 1) * HEAD_DIM]
            outs.append(_dot(probs[sb, h], vg))
        ya_blocks.append(jnp.concatenate(outs, axis=1))

        cvn = cvn_ref[rows, :]
        parts = [_dot(wms[g], cvn[:, g * C_GDIM:(g + 1) * C_GDIM]) for g in range(C_GROUPS)]
        yc_blocks.append(cu_ref[rows, :] * (jnp.concatenate(parts, axis=1) + bs_ref[...]))

    ya = jnp.concatenate(ya_blocks, axis=0)
    yc = jnp.concatenate(yc_blocks, axis=0)
    x1_ref[...] = _merge_and_norm(x_ref[...], ya, yc, gate_ref[...], wa_ref, wc_ref, wo_ref,
                                  g_ref, b_ref)


def _prompt_blocks(nb, sinks, q, kv, cu, cvn, gate, x, bias, ws, bs_exp,
                   wa, wc, wo, ln_g, ln_b):
    T = q.shape[0]
    blk = ATT_BLOCK
    tm = PROMPT_BLOCKS_PER_STEP * blk
    assert nb % PROMPT_BLOCKS_PER_STEP == 0
    row = lambda n: pl.BlockSpec((tm, n), lambda i: (i, 0))
    full = lambda a: pl.BlockSpec(a.shape, lambda i: (0,) * a.ndim)
    prev = pl.BlockSpec((blk, 2 * KV_WIDTH),
                        lambda i: (jnp.maximum(i * PROMPT_BLOCKS_PER_STEP - 1, 0), 0))
    return pl.pallas_call(
        functools.partial(_prompt_block_kernel, nb),
        grid=(T // tm,),
        in_specs=[pl.BlockSpec(memory_space=pltpu.SMEM),
                  row(Q_WIDTH), prev, row(2 * KV_WIDTH), row(C_WIDTH), row(C_WIDTH),
                  row(2 * D_MODEL), row(D_MODEL), full(bias), full(ws), full(bs_exp),
                  full(wa), full(wc), full(wo), full(ln_g), full(ln_b)],
        out_specs=row(D_MODEL),
        out_shape=jax.ShapeDtypeStruct((T, D_MODEL), F32),
        compiler_params=pltpu.CompilerParams(
            dimension_semantics=("arbitrary",), vmem_limit_bytes=48 * MIB),
        name="prompt_blocks",
    )(sinks, q, kv, kv, cu, cvn, gate, x, bias, ws, bs_exp, wa, wc, wo, ln_g, ln_b)


def _sample_block_kernel(seqs, s_new, sinks_ref, q_ref, kv_ref, cu_ref, cvn_ref, gate_ref,
                         x_ref, ck_ref, cv_ref, biasc_ref, biasn_ref, wbd_ref, bs_ref, wa_ref,
                         wc_ref, wo_ref, g_ref, b_ref, x1_ref, kout_ref, vout_ref):
    w_buf = ck_ref.shape[1]
    rows = REP * s_new
    qi_c = lax.rem(lax.broadcasted_iota(I32, (rows, w_buf), 0), s_new)
    kj_c = lax.broadcasted_iota(I32, (rows, w_buf), 1)
    mask_c = (qi_c + w_buf - kj_c) < WINDOW
    qi_n = lax.rem(lax.broadcasted_iota(I32, (rows, s_new), 0), s_new)
    kj_n = lax.broadcasted_iota(I32, (rows, s_new), 1)
    mask_n = kj_n <= qi_n
    sink_col = [jnp.concatenate([jnp.full((s_new, 1), sinks_ref[g * REP + r], F32)
                                 for r in range(REP)], axis=0) for g in range(N_KV)]

    q_all = q_ref[...]
    kv_all = kv_ref[...]
    scale = HEAD_DIM ** -0.5
    work = [(b, g) for b in range(seqs) for g in range(N_KV)]
    ck, cv, kvb, scores = {}, {}, {}, {}
    for b in range(seqs):
        kvb[b] = kv_all[b * s_new:(b + 1) * s_new]
        ck[b] = ck_ref[b]
        cv[b] = cv_ref[b]
        kout_ref[b] = jnp.concatenate([ck[b][s_new:], kvb[b][:, :KV_WIDTH]], axis=0)
        vout_ref[b] = jnp.concatenate([cv[b][s_new:], kvb[b][:, KV_WIDTH:]], axis=0)
    for b, g in work:
        qb = q_all[b * s_new:(b + 1) * s_new]
        qg = jnp.concatenate([qb[:, (g * REP + r) * HEAD_DIM:(g * REP + r + 1) * HEAD_DIM]
                              for r in range(REP)], axis=0)
        lane = slice(g * HEAD_DIM, (g + 1) * HEAD_DIM)
        sc = _dot_nt(qg, ck[b][:, lane]) * scale + biasc_ref[g]
        sn = _dot_nt(qg, kvb[b][:, lane]) * scale + biasn_ref[g]
        scores[b, g] = (jnp.where(mask_c, sc, NEG), jnp.where(mask_n, sn, NEG))
    probs = {}
    for b, g in work:
        sc, sn = scores[b, g]
        sink = sink_col[g]
        m = jnp.maximum(jnp.maximum(jnp.max(sc, axis=-1, keepdims=True),
                                    jnp.max(sn, axis=-1, keepdims=True)), sink)
        ec = jnp.exp(sc - m)
        en = jnp.exp(sn - m)
        den = (jnp.sum(ec, axis=-1, keepdims=True) + jnp.sum(en, axis=-1, keepdims=True)
               + jnp.exp(sink - m))
        probs[b, g] = (ec / den, en / den)
    ya_rows = []
    for b in range(seqs):
        heads = []
        for g in range(N_KV):
            lane = slice(g * HEAD_DIM, (g + 1) * HEAD_DIM)
            vlane = slice(KV_WIDTH + g * HEAD_DIM, KV_WIDTH + (g + 1) * HEAD_DIM)
            pc, pn = probs[b, g]
            og = _dot(pc, cv[b][:, lane]) + _dot(pn, kvb[b][:, vlane])
            heads.extend(og[r * s_new:(r + 1) * s_new] for r in range(REP))
        ya_rows.append(jnp.concatenate(heads, axis=1))
    ya = jnp.concatenate(ya_rows, axis=0)

    cvn = cvn_ref[...]
    parts = [_dot(wbd_ref[g], cvn[:, g * C_GDIM:(g + 1) * C_GDIM]) for g in range(C_GROUPS)]
    yc = cu_ref[...] * (jnp.concatenate(parts, axis=1) + bs_ref[...])

    x1_ref[...] = _merge_and_norm(x_ref[...], ya, yc, gate_ref[...], wa_ref, wc_ref, wo_ref,
                                  g_ref, b_ref)


def _sample_blocks(seqs, s_new, sinks, q, kv, cu, cvn, gate, x, cache_k,
                   cache_v, bias_c, bias_n, wbd, bs_exp, wa, wc, wo, ln_g, ln_b):
    T = q.shape[0]
    n_seq, w_buf, kvw = cache_k.shape
    tm = seqs * s_new
    row = lambda n: pl.BlockSpec((tm, n), lambda i: (i, 0))
    full = lambda a: pl.BlockSpec(a.shape, lambda i: (0,) * a.ndim)
    cache = pl.BlockSpec((seqs, w_buf, kvw), lambda i: (i, 0, 0))
    return pl.pallas_call(
        functools.partial(_sample_block_kernel, seqs, s_new),
        grid=(T // tm,),
        in_specs=[pl.BlockSpec(memory_space=pltpu.SMEM),
                  row(Q_WIDTH), row(2 * KV_WIDTH), row(C_WIDTH), row(C_WIDTH),
                  row(2 * D_MODEL), row(D_MODEL), cache, cache, full(bias_c), full(bias_n),
                  full(wbd), full(bs_exp), full(wa), full(wc), full(wo), full(ln_g),
                  full(ln_b)],
        out_specs=[row(D_MODEL), cache, cache],
        out_shape=[jax.ShapeDtypeStruct((T, D_MODEL), F32),
                   jax.ShapeDtypeStruct(cache_k.shape, F32),
                   jax.ShapeDtypeStruct(cache_v.shape, F32)],
        compiler_params=pltpu.CompilerParams(
            dimension_semantics=("arbitrary",), vmem_limit_bytes=48 * MIB),
        name="sample_blocks",
    )(sinks, q, kv, cu, cvn, gate, x, cache_k, cache_v, bias_c, bias_n, wbd, bs_exp, wa, wc,
      wo, ln_g, ln_b)


def _two_part_specs(tiles_a, tile_rows, width):
    first = pl.BlockSpec((tile_rows, width), lambda i: (jnp.minimum(i, tiles_a - 1), 0))
    second = pl.BlockSpec((tile_rows, width), lambda i: (jnp.maximum(i - tiles_a, 0), 0))
    return first, second


def _two_part_tile(tiles_a, a_ref, b_ref):
    return jnp.where(pl.program_id(0) < tiles_a, a_ref[...], b_ref[...])


def _top16(s, iota, fill):
    vals, idxs = [], []
    for _ in range(PEER_TOPK):
        m = jnp.max(s, axis=0, keepdims=True)
        i = jnp.min(jnp.where(s == m, iota, fill), axis=0, keepdims=True)
        vals.append(m)
        idxs.append(i)
        s = jnp.where(iota == i, -jnp.inf, s)
    return jnp.concatenate(vals, axis=0), jnp.concatenate(idxs, axis=0)


def _pick(table, iota, idx):
    return jnp.concatenate(
        [jnp.sum(jnp.where(iota == idx[r:r + 1], table, 0), axis=0, keepdims=True)
         for r in range(PEER_TOPK)], axis=0)


_CAND_SHORT = SUBLANES


def _cand_flat_index(tt):
    i16 = lax.broadcasted_iota(I32, (PEER_TOPK, tt), 0)
    i8 = lax.broadcasted_iota(I32, (_CAND_SHORT, tt), 0)
    pieces = [i16] + [k * PEER_TOPK + i8 for k in range(1, _CAND_SHORT)]
    pieces.append((_CAND_SHORT + i8) * PEER_TOPK)
    return jnp.concatenate(pieces, axis=0)


def _cand_values(sv0, sv1):
    pieces = [sv0[0:1] + sv1]
    pieces += [sv0[k:k + 1] + sv1[0:_CAND_SHORT] for k in range(1, _CAND_SHORT)]
    pieces.append(sv0[_CAND_SHORT:] + sv1[0:1])
    return jnp.concatenate(pieces, axis=0)


def _route_kernel(tiles_a, xa_ref, xb_ref, wq_ref, sk_ref, e_ref, g_ref, q_scr, et_scr, gt_scr):
    tt = xa_ref.shape[0]
    x1 = _two_part_tile(tiles_a, xa_ref, xb_ref)
    qf = jnp.dot(x1.astype(BF16), wq_ref[...], preferred_element_type=F32)
    for hc in range(2 * PEER_HEADS):
        q_scr[hc] = qf[:, hc * PEER_DHALF:(hc + 1) * PEER_DHALF]
    iota_n = lax.broadcasted_iota(I32, (N_KEYS, tt), 0)
    iota_k = lax.broadcasted_iota(I32, (PEER_TOPK, tt), 0)
    flat = _cand_flat_index(tt)

    def first_stage(h):
        out = []
        for c in range(2):
            s = _dot_nt(sk_ref[h, c], q_scr[2 * h + c])
            out.extend(_top16(s, iota_n, N_KEYS))
        return out[0], out[2], out[1], out[3]

    def second_stage(h, carry):
        sv0, sv1, si0, si1 = carry
        fv, fi = _top16(_cand_values(sv0, sv1), flat, PEER_TOPK * PEER_TOPK)
        i1 = _pick(si0, iota_k, lax.shift_right_logical(fi, 4))
        i2 = _pick(si1, iota_k, lax.bitwise_and(fi, PEER_TOPK - 1))
        ex = jnp.exp(fv - fv[0:1])
        row0 = pl.multiple_of(h * PEER_TOPK, PEER_TOPK)
        et_scr[pl.ds(row0, PEER_TOPK), :] = (i1 * N_KEYS + i2) * PACK_CHUNKS
        gt_scr[pl.ds(row0, PEER_TOPK), :] = ex / jnp.sum(ex, axis=0, keepdims=True)

    def body(h, carry):
        second_stage(h - 1, carry)
        return first_stage(h)

    last = lax.fori_loop(1, PEER_HEADS, body, first_stage(0))
    second_stage(PEER_HEADS - 1, last)
    e_ref[...] = et_scr[...].T
    g_ref[...] = gt_scr[...].T


def _route(x1a, x1b, wq_bf16, subkeys, tt):
    T = x1a.shape[0] + x1b.shape[0]
    tiles_a = x1a.shape[0] // tt
    full = lambda a: pl.BlockSpec(a.shape, lambda i: (0,) * a.ndim)
    out = pl.BlockSpec((tt, PEER_PICKS), lambda i: (i, 0))
    return pl.pallas_call(
        functools.partial(_route_kernel, tiles_a),
        grid=(T // tt,),
        in_specs=[*_two_part_specs(tiles_a, tt, D_MODEL), full(wq_bf16), full(subkeys)],
        out_specs=[out, out],
        out_shape=[jax.ShapeDtypeStruct((T, PEER_PICKS), I32),
                   jax.ShapeDtypeStruct((T, PEER_PICKS), F32)],
        scratch_shapes=[pltpu.VMEM((2 * PEER_HEADS, tt, PEER_DHALF), F32),
                        pltpu.VMEM((PEER_PICKS, tt), I32), pltpu.VMEM((PEER_PICKS, tt), F32)],
        compiler_params=pltpu.CompilerParams(
            dimension_semantics=("arbitrary",), vmem_limit_bytes=48 * MIB),
        name="peer_route",
    )(x1a, x1b, wq_bf16, subkeys)


def _pack_kernel(t_ref, o_ref):
    half = D_MODEL // 2
    o_ref[...] = pltpu.pack_elementwise([t_ref[:, :half], t_ref[:, half:]], packed_dtype=BF16)


def _pack_table(tab, tm):
    n = tab.shape[0]
    packed = pl.pallas_call(
        _pack_kernel,
        grid=(n // tm,),
        in_specs=[pl.BlockSpec((tm, D_MODEL), lambda i: (i, 0))],
        out_specs=pl.BlockSpec((tm, D_MODEL // 2), lambda i: (i, 0)),
        out_shape=jax.ShapeDtypeStruct((n, D_MODEL // 2), jnp.uint32),
        compiler_params=pltpu.CompilerParams(dimension_semantics=("arbitrary",)),
        name="pack_table",
    )(tab)
    return packed.reshape(n * PACK_CHUNKS, LANES)


def _unpack(word):
    return (pltpu.unpack_elementwise(word, index=0, packed_dtype=BF16, unpacked_dtype=F32),
            pltpu.unpack_elementwise(word, index=1, packed_dtype=BF16, unpacked_dtype=F32))


def _gather_group(idx_ref, tab_ref, tile_ref):
    for j in range(PEER_PICKS):
        for t in range(PEER_GROUP):
            row = pl.multiple_of(idx_ref[t * PEER_PICKS + j], PACK_CHUNKS)
            dst = (j * PEER_GROUP + t) * PACK_CHUNKS
            tile_ref[pl.ds(dst, PACK_CHUNKS), :] = tab_ref[pl.ds(row, PACK_CHUNKS), :]


def _tile_rows(tile_ref, c, j):
    start = j * PEER_GROUP * PACK_CHUNKS + c
    return tile_ref[pl.ds(start, PEER_GROUP, stride=PACK_CHUNKS), :]


def _index_copy(e_hbm, group, buf, sem, slot):
    return pltpu.make_async_copy(e_hbm.at[group], buf, sem.at[slot])


def _for_each_group(e_hbm, idx_bufs, sem, groups_per_step, process, groups_per_block=1):
    per_iter = 2 * groups_per_block
    assert len(idx_bufs) == per_iter and groups_per_step % per_iter == 0
    step = pl.program_id(0)
    total = pl.num_programs(0) * groups_per_step
    base = step * groups_per_step

    @pl.when(step == 0)
    def _():
        for slot in range(per_iter):
            _index_copy(e_hbm, slot, idx_bufs[slot], sem, slot).start()

    def body(it, carry):
        first = base + per_iter * it
        for b in range(2):
            slots = range(b * groups_per_block, (b + 1) * groups_per_block)
            for slot in slots:
                _index_copy(e_hbm, first + slot, idx_bufs[slot], sem, slot).wait()
            for k, slot in enumerate(slots):
                process(idx_bufs[slot], per_iter * it + slot, k)

            @pl.when(first + per_iter + slots[0] < total)
            def _():
                for slot in slots:
                    _index_copy(e_hbm, first + per_iter + slot, idx_bufs[slot], sem, slot).start()
        return carry

    lax.fori_loop(0, groups_per_step // per_iter, body, 0)


U_GROUPS_PER_BLOCK = 4


def _peer_u_kernel(tiles_a, e_hbm, xa_ref, xb_ref, g_ref, tab_ref, w_ref, x_ref, *scratch):
    tiles = scratch[:U_GROUPS_PER_BLOCK]
    idx_bufs, sem = scratch[U_GROUPS_PER_BLOCK:-1], scratch[-1]
    lane = lax.broadcasted_iota(I32, (PEER_GROUP, PEER_PICKS), 1)
    half = D_MODEL // 2
    x_ref[...] = _two_part_tile(tiles_a, xa_ref, xb_ref)

    def process(idx_ref, local, k):
        tile_ref = tiles[k]
        t0 = pl.multiple_of(local * PEER_GROUP, PEER_GROUP)
        _gather_group(idx_ref, tab_ref, tile_ref)
        xg = x_ref[pl.ds(t0, PEER_GROUP), :]
        x_lo = [xg[:, c * LANES:(c + 1) * LANES] for c in range(PACK_CHUNKS)]
        x_hi = [xg[:, half + c * LANES:half + (c + 1) * LANES] for c in range(PACK_CHUNKS)]
        a = jnp.zeros((PEER_GROUP, PEER_PICKS), F32)
        for j in range(PEER_PICKS):
            p = None
            for c in range(PACK_CHUNKS):
                lo, hi = _unpack(_tile_rows(tile_ref, c, j))
                term = lo * x_lo[c] + hi * x_hi[c]
                p = term if p is None else p + term
            a = jnp.where(lane == j, jnp.sum(p, axis=1, keepdims=True), a)
        w_ref[pl.ds(t0, PEER_GROUP), :] = g_ref[pl.ds(t0, PEER_GROUP), :] * jax.nn.gelu(a)

    _for_each_group(e_hbm, idx_bufs, sem, x_ref.shape[0] // PEER_GROUP, process,
                    U_GROUPS_PER_BLOCK)


def _peer_v_kernel(e_hbm, w_ref, tab_ref, f_ref, tile_ref, idx_a, idx_b, sem):
    def process(idx_ref, local, k):
        t0 = pl.multiple_of(local * PEER_GROUP, PEER_GROUP)
        _gather_group(idx_ref, tab_ref, tile_ref)
        wg = w_ref[pl.ds(t0, PEER_GROUP), :]
        acc = [jnp.zeros((PEER_GROUP, LANES), F32) for _ in range(2 * PACK_CHUNKS)]
        for j in range(PEER_PICKS):
            wb = jnp.broadcast_to(wg[:, j:j + 1], (PEER_GROUP, LANES))
            for c in range(PACK_CHUNKS):
                lo, hi = _unpack(_tile_rows(tile_ref, c, j))
                acc[c] = acc[c] + wb * lo
                acc[PACK_CHUNKS + c] = acc[PACK_CHUNKS + c] + wb * hi
        f_ref[pl.ds(t0, PEER_GROUP), :] = jnp.concatenate(acc, axis=1)

    _for_each_group(e_hbm, (idx_a, idx_b), sem, w_ref.shape[0] // PEER_GROUP, process)


def _peer_specs(tb, groups_per_block):
    idx = pl.BlockSpec(memory_space=pl.ANY)
    picks = pl.BlockSpec((tb, PEER_PICKS), lambda i: (i, 0))
    feat = pl.BlockSpec((tb, D_MODEL), lambda i: (i, 0))
    table = pl.BlockSpec((N_EXPERTS * PACK_CHUNKS, LANES), lambda i: (0, 0),
                         pipeline_mode=pl.Buffered(1))
    n_idx = 2 * groups_per_block
    scratch = ([pltpu.VMEM((PACK_CHUNKS * PEER_ROWS, LANES), jnp.uint32)] * groups_per_block
               + [pltpu.SMEM((PEER_ROWS,), I32)] * n_idx + [pltpu.SemaphoreType.DMA((n_idx,))])
    params = pltpu.CompilerParams(dimension_semantics=("arbitrary",),
                                  vmem_limit_bytes=48 * MIB)
    return idx, picks, feat, table, scratch, params


def _peer_u(e_grp, x1a, x1b, g, tab_u, tb):
    T = x1a.shape[0] + x1b.shape[0]
    tiles_a = x1a.shape[0] // tb
    idx, picks, feat, table, scratch, params = _peer_specs(tb, U_GROUPS_PER_BLOCK)
    return pl.pallas_call(
        functools.partial(_peer_u_kernel, tiles_a), grid=(T // tb,),
        in_specs=[idx, *_two_part_specs(tiles_a, tb, D_MODEL), picks, table], out_specs=picks,
        out_shape=jax.ShapeDtypeStruct((T, PEER_PICKS), F32),
        scratch_shapes=[pltpu.VMEM((tb, D_MODEL), F32)] + scratch,
        compiler_params=params,
        name="peer_u",
    )(e_grp, x1a, x1b, g, tab_u)


def _peer_v(e_grp, w, tab_v, tb):
    T = w.shape[0]
    idx, picks, feat, table, scratch, params = _peer_specs(tb, 1)
    return pl.pallas_call(
        _peer_v_kernel, grid=(T // tb,),
        in_specs=[idx, picks, table], out_specs=feat,
        out_shape=jax.ShapeDtypeStruct((T, D_MODEL), F32),
        scratch_shapes=scratch,
        compiler_params=params, name="peer_v",
    )(e_grp, w, tab_v)


def _final_kernel(x1_ref, f_ref, p_ref, wg_ref, wp_ref, g_ref, b_ref, y_ref):
    x1 = x1_ref[...]
    e = jax.nn.sigmoid(_dot(x1, wg_ref[...])) * _dot(p_ref[...], wp_ref[...])
    y_ref[...] = _layer_norm(ALPHA * x1 + f_ref[...] + e, g_ref[...], b_ref[...])


def _final(x1, f, f_row_offset, p, wg, wp, ln_g, ln_b, tm):
    T = p.shape[0]
    off = f_row_offset // tm
    full = lambda a: pl.BlockSpec(a.shape, lambda i: (0,) * a.ndim)
    shifted = pl.BlockSpec((tm, D_MODEL), lambda i: (i + off, 0))
    return pl.pallas_call(
        _final_kernel,
        grid=(T // tm,),
        in_specs=[pl.BlockSpec((tm, D_MODEL), lambda i: (i, 0)), shifted,
                  pl.BlockSpec((tm, PLE_DIM), lambda i: (i, 0)),
                  full(wg), full(wp), full(ln_g), full(ln_b)],
        out_specs=pl.BlockSpec((tm, D_MODEL), lambda i: (i, 0)),
        out_shape=jax.ShapeDtypeStruct((T, D_MODEL), F32),
        compiler_params=pltpu.CompilerParams(
            dimension_semantics=("arbitrary",), vmem_limit_bytes=48 * MIB),
        name="final",
    )(x1, f, p, wg, wp, ln_g, ln_b)


def _t5_bias_by_distance(dist, table):
    n = jnp.maximum(dist, 0)
    max_exact = N_BUCKETS // 2
    nf = jnp.maximum(n, 1).astype(F32)
    large = max_exact + jnp.floor(jnp.log(nf / max_exact) / math.log(MAX_DISTANCE / max_exact)
                                  * (N_BUCKETS - max_exact)).astype(I32)
    large = jnp.minimum(large, N_BUCKETS - 1)
    bucket = jnp.where(n < max_exact, n, large)
    hit = bucket[None, :, None] == jnp.arange(N_BUCKETS)
    return jnp.sum(jnp.where(hit, table.T[:, None, :], 0.0), axis=-1)


def _band_bias(table, n_query, n_key, key_offset):
    d_max = n_query - 1 + key_offset
    d_min = key_offset - (n_key - 1)
    by_dist = _t5_bias_by_distance(jnp.arange(d_max, d_min - 1, -1), table)
    rows = [by_dist[:, d_max - (q + key_offset):d_max - (q + key_offset) + n_key]
            for q in range(n_query)]
    return jnp.stack(rows, axis=1)


def kernel(x_prompt, x_sample, cache_k_win, cache_v_win, p_prompt, p_sample, rel_bias_table,
           w_in, attn_sinks, w_att_out, c_ln_g, c_ln_b, c_ws, c_bs, w_chunk_out, w_o, ln1_g,
           ln1_b, peer_wq, peer_subkeys, peer_u, peer_v, w_ple_gate, w_ple_proj, ln2_g, ln2_b):
    batch, seq, d = x_prompt.shape
    n_seq, s_new, _ = x_sample.shape
    w_buf = cache_k_win.shape[2]
    assert d == D_MODEL and w_in.shape[0] == DEPTH and seq % ATT_BLOCK == 0
    tp = batch * seq
    ts = n_seq * s_new
    t_all = tp + ts
    nb = seq // ATT_BLOCK
    seqs_per_step = 8
    assert n_seq % seqs_per_step == 0 and tp % (seqs_per_step * s_new) == 0
    assert tp % 256 == 0 and ts % 256 == 0 and tp % PEER_TOKENS_PER_STEP == 0
    assert tp % ROUTE_TILE == 0 and ts % ROUTE_TILE == 0

    row2 = lambda v: v.reshape(1, -1)
    w_in_b = w_in[0].astype(BF16)
    wa, wc, wo = w_att_out[0].astype(BF16), w_chunk_out[0].astype(BF16), w_o[0].astype(BF16)
    sinks = attn_sinks[0]
    cg, cb = row2(c_ln_g[0]), row2(c_ln_b[0])
    l1g, l1b = row2(ln1_g[0]), row2(ln1_b[0])
    l2g, l2b = row2(ln2_g[0]), row2(ln2_b[0])
    ws, bs = c_ws[0], c_bs[0]

    bias_p = _band_bias(rel_bias_table, ATT_BLOCK, 2 * ATT_BLOCK, ATT_BLOCK)
    bias_s = _band_bias(rel_bias_table, s_new, w_buf + s_new, w_buf)
    bias_s = bias_s.reshape(N_KV, REP * s_new, w_buf + s_new)
    bias_sc, bias_sn = bias_s[:, :, :w_buf], bias_s[:, :, w_buf:]

    bs_exp_p = jnp.repeat(bs.T, C_GDIM, axis=1)
    bs_exp_s = jnp.tile(jnp.repeat(bs[:, :s_new].T, C_GDIM, axis=1), (seqs_per_step, 1))
    tril = jnp.tril(jnp.ones((s_new, s_new), F32))
    eye = jnp.eye(seqs_per_step, dtype=F32)
    wbd = jnp.stack([jnp.kron(eye, ws[g, :s_new, :s_new] * tril) for g in range(C_GROUPS)])

    xp = x_prompt.reshape(tp, d)
    q, kv, cu, cvn, gate = _inproj(xp, w_in_b, cg, cb, 256)
    x1p = _prompt_blocks(nb, sinks, q, kv, cu, cvn, gate, xp, bias_p, ws, bs_exp_p,
                        wa, wc, wo, l1g, l1b)
    kv_tail = kv.reshape(batch, seq, 2 * KV_WIDTH)[:, seq - w_buf:]
    kp = kv_tail[..., :KV_WIDTH].reshape(1, batch, w_buf, N_KV, HEAD_DIM)
    vp = kv_tail[..., KV_WIDTH:].reshape(1, batch, w_buf, N_KV, HEAD_DIM)

    xs = x_sample.reshape(ts, d)
    q, kv, cu, cvn_s, gate = _inproj(xs, w_in_b, cg, cb, 256)
    x1s, k_new, v_new = _sample_blocks(
        seqs_per_step, s_new, sinks, q, kv, cu, cvn_s, gate, xs,
        cache_k_win[0].reshape(n_seq, w_buf, KV_WIDTH),
        cache_v_win[0].reshape(n_seq, w_buf, KV_WIDTH),
        bias_sc, bias_sn, wbd, bs_exp_s, wa, wc, wo, l1g, l1b)
    ks_out = k_new.reshape(1, n_seq, w_buf, N_KV, HEAD_DIM)
    vs_out = v_new.reshape(1, n_seq, w_buf, N_KV, HEAD_DIM)
    cs_out = cvn_s.reshape(1, n_seq, s_new, C_WIDTH)

    e_tok, g_tok = _route(x1p, x1s, peer_wq[0].astype(BF16), peer_subkeys[0], ROUTE_TILE)
    e_grp = e_tok.reshape(t_all // PEER_GROUP, PEER_ROWS)
    tab_u = _pack_table(peer_u[0], 1024)
    tab_v = _pack_table(peer_v[0], 1024)
    w_tok = _peer_u(e_grp, x1p, x1s, g_tok, tab_u, PEER_TOKENS_PER_STEP)
    f = _peer_v(e_grp, w_tok, tab_v, PEER_TOKENS_PER_STEP)

    wg, wp = w_ple_gate[0].astype(BF16), w_ple_proj[0].astype(BF16)
    yp = _final(x1p, f, 0, p_prompt[0].reshape(tp, PLE_DIM), wg, wp, l2g, l2b,
                512 if tp % 512 == 0 else 256)
    ys = _final(x1s, f, tp, p_sample[0].reshape(ts, PLE_DIM), wg, wp, l2g, l2b, 256)
    return (yp.reshape(batch, seq, d), ys.reshape(n_seq, s_new, d), kp, vp, ks_out, vs_out,
            cs_out)
```
